```python
import math
import jax, jax.numpy as jnp
from jax import lax
import numpy as np

D_MODEL = 2048
BATCH = 8
SEQ = 2048
DEPTH = 2

ML_HEADS = 4
ML_HEAD_DIM = 256
ML_WIDTH = ML_HEADS * ML_HEAD_DIM
SB_HEADS = 8
SB_HEAD_DIM = 128
SB_WIDTH = SB_HEADS * SB_HEAD_DIM
MIX_WIDTH = ML_WIDTH + SB_WIDTH
IN_COLS = 4 * ML_WIDTH + 2 * ML_HEADS + 3 * SB_WIDTH
QK_CONV = 4
ML_CHUNK = 128
SB_BLOCK = 128
CONV_WIDTH = 31
CONF_WIDTH = D_MODEL
D_FF = 5632
N_EXPERTS = 8
TOP_K = 2
D_FF_EXPERT = 7168
MOE_BLOCK = 256
EPS = 1e-6
N_EVEN = (DEPTH + 1) // 2
N_ODD = DEPTH // 2

kernel_name = "hybrid_mlstm_stickbreak_conformer_moe"


def rms_norm(x, g):
    x32 = x.astype(jnp.float32)
    y = x32 * lax.rsqrt(jnp.mean(x32 * x32, axis=-1, keepdims=True) + EPS)
    return (y * g.astype(jnp.float32)).astype(x.dtype)


def layer_norm(x, g, b):
    x32 = x.astype(jnp.float32)
    mu = jnp.mean(x32, axis=-1, keepdims=True)
    xc = x32 - mu
    y = xc * lax.rsqrt(jnp.mean(xc * xc, axis=-1, keepdims=True) + EPS)
    return (y * g.astype(jnp.float32) + b.astype(jnp.float32)).astype(x.dtype)


def causal_depthwise_conv(x, w):
    width, ch = w.shape
    return lax.conv_general_dilated(
        x, w[:, None, :].astype(x.dtype), window_strides=(1,),
        padding=[(width - 1, 0)], dimension_numbers=("NWC", "WIO", "NWC"),
        feature_group_count=ch)


def to_heads(t, n_heads):
    b, s, _ = t.shape
    return t.reshape(b, s, n_heads, -1).transpose(0, 2, 1, 3).astype(jnp.float32)


def from_heads(t):
    b, h, s, d = t.shape
    return t.transpose(0, 2, 1, 3).reshape(b, s, h * d)


def mlstm_chunkwise(q, k, v, i_pre, f_pre):
    b, nh, s, d = q.shape
    L = ML_CHUNK
    nc = s // L
    q = q.reshape(b, nh, nc, L, d)
    k = k.reshape(b, nh, nc, L, d) * (d ** -0.5)
    v = v.reshape(b, nh, nc, L, d)
    log_f = jax.nn.log_sigmoid(f_pre).reshape(b, nh, nc, L)
    log_i = i_pre.reshape(b, nh, nc, L)
    bl = jnp.cumsum(log_f, axis=-1)
    g = bl[..., -1]
    a = g[..., None] - bl + log_i
    m_loc = jnp.max(a, axis=-1)
    w_loc = jnp.exp(a - m_loc[..., None])
    c_loc = jnp.einsum("bhcld,bhcle->bhcde", k * w_loc[..., None], v)
    n_loc = jnp.einsum("bhcl,bhcld->bhcd", w_loc, k)

    def step(carry, xs):
        c_st, n_st, m_st = carry
        g_c, m_l, c_l, n_l = xs
        m_new = jnp.maximum(g_c + m_st, m_l)
        s_prev = jnp.exp(g_c + m_st - m_new)
        s_loc = jnp.exp(m_l - m_new)
        c_new = s_prev[..., None, None] * c_st + s_loc[..., None, None] * c_l
        n_new = s_prev[..., None] * n_st + s_loc[..., None] * n_l
        return (c_new, n_new, m_new), (c_st, n_st, m_st)

    init = (jnp.zeros((b, nh, d, d), jnp.float32),
            jnp.zeros((b, nh, d), jnp.float32),
            jnp.zeros((b, nh), jnp.float32))
    xs = (jnp.moveaxis(g, 2, 0), jnp.moveaxis(m_loc, 2, 0),
          jnp.moveaxis(c_loc, 2, 0), jnp.moveaxis(n_loc, 2, 0))
    _, (c_prev, n_prev, m_prev) = lax.scan(step, init, xs)
    c_prev = jnp.moveaxis(c_prev, 0, 2)
    n_prev = jnp.moveaxis(n_prev, 0, 2)
    m_prev = jnp.moveaxis(m_prev, 0, 2)

    causal = jnp.tril(jnp.ones((L, L), dtype=bool))
    d_log = bl[..., :, None] - bl[..., None, :] + log_i[..., None, :]
    d_log = jnp.where(causal, d_log, -jnp.inf)
    inter_log = bl + m_prev[..., None]
    m_t = jnp.maximum(inter_log, jnp.max(d_log, axis=-1))
    w_intra = jnp.exp(d_log - m_t[..., None])
    w_inter = jnp.exp(inter_log - m_t)
    s_qk = jnp.einsum("bhcld,bhcsd->bhcls", q, k) * w_intra
    num = (w_inter[..., None] * jnp.einsum("bhcld,bhcde->bhcle", q, c_prev)
           + jnp.einsum("bhcls,bhcse->bhcle", s_qk, v))
    qn = w_inter * jnp.einsum("bhcld,bhcd->bhcl", q, n_prev) + jnp.sum(s_qk, axis=-1)
    denom = jnp.maximum(jnp.abs(qn), jnp.exp(-m_t))
    out = num / denom[..., None]
    return out.reshape(b, nh, s, d)


def stick_breaking_attention(q, k, v):
    s = q.shape[2]
    scale = q.shape[-1] ** -0.5
    outs = []
    for blk in range(s // SB_BLOCK):
        start = blk * SB_BLOCK
        end = start + SB_BLOCK
        z = jnp.einsum("bhqd,bhkd->bhqk", q[:, :, start:end], k[:, :, :end]) * scale
        t_pos = start + jnp.arange(SB_BLOCK)[:, None]
        s_pos = jnp.arange(end)[None, :]
        strict = s_pos < t_pos
        log_not = jnp.where(strict, -jax.nn.softplus(z), 0.0)
        rest = lax.cumsum(log_not, axis=3, reverse=True) - log_not
        weights = jnp.where(strict, jnp.exp(jax.nn.log_sigmoid(z) + rest), 0.0)
        outs.append(jnp.einsum("bhqk,bhkd->bhqd", weights, v[:, :, :end]))
    return jnp.concatenate(outs, axis=2)


def mlstm_stickbreak_mixer(x, w_in, conv_qk, b_gates, ml_norm, w_out):
    proj = x @ w_in
    ml_qk, ml_v, ml_o, ml_gates, sb_qkv = jnp.split(
        proj, [2 * ML_WIDTH, 3 * ML_WIDTH, 4 * ML_WIDTH, 4 * ML_WIDTH + 2 * ML_HEADS], axis=-1)
    ml_qk = jax.nn.silu(causal_depthwise_conv(ml_qk, conv_qk))
    ml_q, ml_k = jnp.split(ml_qk, 2, axis=-1)
    gates = ml_gates.astype(jnp.float32) + b_gates.astype(jnp.float32)
    i_pre, f_pre = jnp.split(gates, 2, axis=-1)
    h_ml = mlstm_chunkwise(to_heads(ml_q, ML_HEADS), to_heads(ml_k, ML_HEADS),
                           to_heads(ml_v, ML_HEADS),
                           i_pre.transpose(0, 2, 1), f_pre.transpose(0, 2, 1))
    h_ml = h_ml * lax.rsqrt(jnp.mean(h_ml * h_ml, axis=-1, keepdims=True) + EPS)
    h_ml = from_heads(h_ml) * ml_norm.astype(jnp.float32) * jax.nn.sigmoid(ml_o.astype(jnp.float32))
    sb_q, sb_k, sb_v = jnp.split(sb_qkv, 3, axis=-1)
    h_sb = from_heads(stick_breaking_attention(to_heads(sb_q, SB_HEADS), to_heads(sb_k, SB_HEADS),
                                               to_heads(sb_v, SB_HEADS)))
    mixed = jnp.concatenate([h_ml.astype(x.dtype), h_sb.astype(x.dtype)], axis=-1)
    return mixed @ w_out


def conformer_conv_module(x, w_pw1, w_dw, ln_g, ln_b, w_pw2):
    u = x @ w_pw1
    a, gate = jnp.split(u, 2, axis=-1)
    u = a * jax.nn.sigmoid(gate)
    u = causal_depthwise_conv(u, w_dw)
    u = jax.nn.silu(layer_norm(u, ln_g, ln_b))
    return u @ w_pw2


def swiglu(x, w1, w3, w2):
    return (jax.nn.silu(x @ w1) * (x @ w3)) @ w2


def moe_swiglu(x, w_router, b_router, w1, w3, w2):
    b, s, dm = x.shape
    xt = x.reshape(-1, dm)
    n_tok = xt.shape[0]
    logits = xt.astype(jnp.float32) @ w_router.astype(jnp.float32) + b_router.astype(jnp.float32)
    top_logit, top_idx = lax.top_k(logits, TOP_K)
    gates = jax.nn.softmax(top_logit, axis=-1)
    n_asg = n_tok * TOP_K
    flat_e = top_idx.reshape(-1)
    flat_tok = jnp.arange(n_asg, dtype=jnp.int32) // TOP_K
    flat_gate = gates.reshape(-1)
    order = jnp.argsort(flat_e)
    sorted_e = flat_e[order]
    counts = jnp.bincount(flat_e, length=N_EXPERTS)
    padded = ((counts + MOE_BLOCK - 1) // MOE_BLOCK) * MOE_BLOCK
    start = jnp.cumsum(counts) - counts
    pad_end = jnp.cumsum(padded)
    pad_start = pad_end - padded
    dest = pad_start[sorted_e] + jnp.arange(n_asg, dtype=jnp.int32) - start[sorted_e]
    n_blocks = -(-n_asg // MOE_BLOCK) + N_EXPERTS
    n_rows = n_blocks * MOE_BLOCK
    row_tok = jnp.full((n_rows,), n_tok, dtype=jnp.int32).at[dest].set(flat_tok[order])
    row_gate = jnp.zeros((n_rows,), jnp.float32).at[dest].set(flat_gate[order])
    block_e = jnp.minimum(
        jnp.searchsorted(pad_end, jnp.arange(n_blocks) * MOE_BLOCK, side="right"), N_EXPERTS - 1)
    x_pad = jnp.concatenate([xt, jnp.zeros((1, dm), xt.dtype)], axis=0)
    x_rows = x_pad[row_tok].reshape(n_blocks, MOE_BLOCK, dm)

    def expert_block(args):
        xb, e = args
        return (jax.nn.silu(xb @ w1[e]) * (xb @ w3[e])) @ w2[e]

    y_rows = lax.map(expert_block, (x_rows, block_e)).reshape(n_rows, dm)
    y_rows = y_rows * row_gate[:, None].astype(y_rows.dtype)
    y = jnp.zeros((n_tok + 1, dm), y_rows.dtype).at[row_tok].add(y_rows)
    return y[:n_tok].reshape(b, s, dm)


def setup_inputs(seed: int = 0) -> dict:
    key = jax.random.key(seed)
    ks = iter(jax.random.split(key, 32))

    def nrm(shape, scale):
        return jax.random.normal(next(ks), shape, jnp.float32) * scale

    def gain(shape):
        return 1.0 + nrm(shape, 0.02)

    ne, no = N_EVEN, N_ODD
    b_gates = jnp.concatenate(
        [nrm((ne, ML_HEADS), 0.1),
         jnp.linspace(3.0, 6.0, ML_HEADS, dtype=jnp.float32)[None, :] + nrm((ne, ML_HEADS), 0.1)],
        axis=-1)
    return {
        "x": nrm((BATCH, SEQ, D_MODEL), 1.0),
        "ev_norm_mix": gain((ne, D_MODEL)),
        "ev_w_in": nrm((ne, D_MODEL, IN_COLS), D_MODEL ** -0.5),
        "ev_conv_qk": nrm((ne, QK_CONV, 2 * ML_WIDTH), QK_CONV ** -0.5),
        "ev_b_gates": b_gates,
        "ev_ml_norm": gain((ne, ML_WIDTH)),
        "ev_w_out": nrm((ne, MIX_WIDTH, D_MODEL), MIX_WIDTH ** -0.5),
        "ev_norm_ffn": gain((ne, D_MODEL)),
        "ev_w1": nrm((ne, D_MODEL, D_FF), D_MODEL ** -0.5),
        "ev_w3": nrm((ne, D_MODEL, D_FF), D_MODEL ** -0.5),
        "ev_w2": nrm((ne, D_FF, D_MODEL), D_FF ** -0.5),
        "od_norm_mix": gain((no, D_MODEL)),
        "od_pw1": nrm((no, D_MODEL, 2 * CONF_WIDTH), D_MODEL ** -0.5),
        "od_dw": nrm((no, CONV_WIDTH, CONF_WIDTH), CONV_WIDTH ** -0.5),
        "od_ln_g": gain((no, CONF_WIDTH)),
        "od_ln_b": nrm((no, CONF_WIDTH), 0.02),
        "od_pw2": nrm((no, CONF_WIDTH, D_MODEL), CONF_WIDTH ** -0.5),
        "od_norm_ffn": gain((no, D_MODEL)),
        "od_router": nrm((no, D_MODEL, N_EXPERTS), D_MODEL ** -0.5),
        "od_router_b": nrm((no, N_EXPERTS), 0.01),
        "od_moe_w1": nrm((no, N_EXPERTS, D_MODEL, D_FF_EXPERT), D_MODEL ** -0.5),
        "od_moe_w3": nrm((no, N_EXPERTS, D_MODEL, D_FF_EXPERT), D_MODEL ** -0.5),
        "od_moe_w2": nrm((no, N_EXPERTS, D_FF_EXPERT, D_MODEL), D_FF_EXPERT ** -0.5),
        "final_norm": gain((D_MODEL,)),
    }


def reference(x, ev_norm_mix, ev_w_in, ev_conv_qk, ev_b_gates, ev_ml_norm, ev_w_out,
              ev_norm_ffn, ev_w1, ev_w3, ev_w2, od_norm_mix, od_pw1, od_dw, od_ln_g,
              od_ln_b, od_pw2, od_norm_ffn, od_router, od_router_b, od_moe_w1,
              od_moe_w3, od_moe_w2, final_norm):
    h = x
    for layer in range(DEPTH):
        j = layer // 2
        if layer % 2 == 0:
            h = h + mlstm_stickbreak_mixer(rms_norm(h, ev_norm_mix[j]), ev_w_in[j], ev_conv_qk[j],
                                           ev_b_gates[j], ev_ml_norm[j], ev_w_out[j])
            h = h + swiglu(rms_norm(h, ev_norm_ffn[j]), ev_w1[j], ev_w3[j], ev_w2[j])
        else:
            h = h + conformer_conv_module(rms_norm(h, od_norm_mix[j]), od_pw1[j], od_dw[j],
                                          od_ln_g[j], od_ln_b[j], od_pw2[j])
            h = h + moe_swiglu(rms_norm(h, od_norm_ffn[j]), od_router[j], od_router_b[j],
                               od_moe_w1[j], od_moe_w3[j], od_moe_w2[j])
    return rms_norm(h, final_norm)
```

```python
import functools

import jax
import jax.numpy as jnp
from jax import lax
from jax.experimental import pallas as pl
from jax.experimental.pallas import tpu as pltpu

ML_HEADS = 4
ML_HEAD_DIM = 256
ML_WIDTH = ML_HEADS * ML_HEAD_DIM
SB_HEADS = 8
SB_HEAD_DIM = 128
SB_WIDTH = SB_HEADS * SB_HEAD_DIM
QK_CONV = 4
ML_CHUNK = 128
CONV_WIDTH = 31
N_EXPERTS = 8
TOP_K = 2
EPS = 1e-6

BF16 = jnp.bfloat16
F32 = jnp.float32
HIGHEST = lax.Precision.HIGHEST

LANES = 128
BF16_SUBLANES = 16
V7X_VMEM_LIMIT = 56 * 1024 * 1024

NT_DIMS = (((1,), (1,)), ((), ()))
TN_DIMS = (((0,), (0,)), ((), ()))


def _params(*semantics):
    return pltpu.CompilerParams(dimension_semantics=semantics, vmem_limit_bytes=V7X_VMEM_LIMIT)


def _rms_norm(x, g):
    return x * lax.rsqrt(jnp.mean(x * x, axis=-1, keepdims=True) + EPS) * g


def _sigmoid(x):
    return 1.0 / (1.0 + jnp.exp(-x))


def _softplus(z):
    return jnp.maximum(z, 0.0) + jnp.log1p(jnp.exp(-jnp.abs(z)))


def _in_proj_kernel(x_ref, g_ref, w_ref, wg_ref, bg_ref, proj_ref, gates_ref, xn_ref):
    @pl.when(pl.program_id(1) == 0)
    def _():
        xn = _rms_norm(x_ref[...], g_ref[...])
        xn_ref[...] = xn.astype(BF16)
        gates_ref[...] = jnp.dot(xn, wg_ref[...], precision=HIGHEST,
                                 preferred_element_type=F32) + bg_ref[...]

    proj_ref[...] = jnp.dot(xn_ref[...], w_ref[...], preferred_element_type=F32).astype(BF16)


def _in_proj(x, g, w, wg, bg, *, tm, tn):
    m, d = x.shape
    n = w.shape[1]
    return pl.pallas_call(
        _in_proj_kernel,
        grid=(m // tm, n // tn),
        in_specs=[
            pl.BlockSpec((tm, d), lambda i, j: (i, 0)),
            pl.BlockSpec((1, d), lambda i, j: (0, 0)),
            pl.BlockSpec((d, tn), lambda i, j: (0, j)),
            pl.BlockSpec((d, LANES), lambda i, j: (0, 0)),
            pl.BlockSpec((1, LANES), lambda i, j: (0, 0)),
        ],
        out_specs=[
            pl.BlockSpec((tm, tn), lambda i, j: (i, j)),
            pl.BlockSpec((tm, LANES), lambda i, j: (i, 0)),
        ],
        out_shape=[jax.ShapeDtypeStruct((m, n), BF16), jax.ShapeDtypeStruct((m, LANES), F32)],
        scratch_shapes=[pltpu.VMEM((tm, d), BF16)],
        compiler_params=_params("parallel", "arbitrary"),
        name="in_proj",
    )(x, g, w, wg, bg)


def _mlstm_kernel(q_ref, k_ref, v_ref, o_ref, cq_ref, ck_ref, i_ref, f_ref, nrm_ref,
                  out_ref, c_state, n_state, m_state):
    L = ML_CHUNK
    d = ML_HEAD_DIM
    n_chunks = q_ref.shape[0] // L
    halo = BF16_SUBLANES

    c_state[...] = jnp.zeros_like(c_state)
    n_state[...] = jnp.zeros_like(n_state)
    m_state[...] = jnp.zeros_like(m_state)

    row = lax.broadcasted_iota(jnp.int32, (L, L), 0)
    col = lax.broadcasted_iota(jnp.int32, (L, L), 1)
    causal = col <= row
    eye = col == row

    def conv_swish(x_ref, w_ref, c):
        start = pl.multiple_of(c * L, L)
        cur = x_ref[pl.ds(start, L), :].astype(F32)
        prev_start = pl.multiple_of(jnp.maximum(start - halo, 0), halo)
        prev = x_ref[pl.ds(prev_start, halo), :].astype(F32)
        prev = jnp.where(c > 0, prev, 0.0)
        xx = jnp.concatenate([prev, cur], axis=0)
        w = w_ref[...]
        y = jnp.zeros((L, d), F32)
        for j in range(QK_CONV):
            off = halo - (QK_CONV - 1) + j
            y = y + w[j:j + 1, :] * xx[off:off + L, :]
        return y * _sigmoid(y)

    def to_col(r):
        return jnp.sum(jnp.where(eye, r, 0.0), axis=1, keepdims=True)

    def chunk(c, carry):
        start = pl.multiple_of(c * L, L)
        q = conv_swish(q_ref, cq_ref, c)
        k = conv_swish(k_ref, ck_ref, c) * (d ** -0.5)
        v = v_ref[pl.ds(start, L), :]
        li_row = i_ref[pl.ds(c, 1), :]
        f_row = f_ref[pl.ds(c, 1), :]
        lf_row = -_softplus(-f_row)
        lf_col = to_col(lf_row)
        li_col = to_col(li_row)
        bl_col = jnp.sum(jnp.where(causal, lf_row, 0.0), axis=1, keepdims=True)
        bl_row = jnp.sum(jnp.where(col >= row, lf_col, 0.0), axis=0, keepdims=True)
        g = jnp.sum(lf_row, axis=1, keepdims=True)

        c_prev = c_state[...]
        n_prev = n_state[...]
        m_prev = m_state[...]

        d_log = jnp.where(causal, bl_col - bl_row + li_row, -jnp.inf)
        inter = bl_col + m_prev
        m_t = jnp.maximum(inter, jnp.max(d_log, axis=1, keepdims=True))
        w_intra = jnp.exp(d_log - m_t)
        w_inter = jnp.exp(inter - m_t)
        q_b = q.astype(BF16)
        k_b = k.astype(BF16)
        s_qk = lax.dot_general(q_b, k_b, NT_DIMS, preferred_element_type=F32) * w_intra
        num = (w_inter * jnp.dot(q_b, c_prev.astype(BF16), preferred_element_type=F32)
               + jnp.dot(s_qk.astype(BF16), v, preferred_element_type=F32))
        qn = (w_inter * jnp.sum(q * n_prev, axis=1, keepdims=True)
              + jnp.sum(s_qk, axis=1, keepdims=True))
        denom = jnp.maximum(jnp.abs(qn), jnp.exp(-m_t))
        h = num / denom
        h = h * lax.rsqrt(jnp.mean(h * h, axis=1, keepdims=True) + EPS)
        h = h * nrm_ref[...] * _sigmoid(o_ref[pl.ds(start, L), :].astype(F32))
        out_ref[pl.ds(start, L), :] = h.astype(out_ref.dtype)

        a_col = g - bl_col + li_col
        m_loc = jnp.max(a_col, axis=0, keepdims=True)
        kw = k * jnp.exp(a_col - m_loc)
        c_loc = lax.dot_general(kw.astype(BF16), v, TN_DIMS, preferred_element_type=F32)
        n_loc = jnp.sum(kw, axis=0, keepdims=True)
        m_new = jnp.maximum(g + m_prev, m_loc)
        s_prev = jnp.exp(g + m_prev - m_new)
        s_loc = jnp.exp(m_loc - m_new)
        c_state[...] = s_prev * c_prev + s_loc * c_loc
        n_state[...] = s_prev * n_prev + s_loc * n_loc
        m_state[...] = m_new
        return carry

    lax.fori_loop(0, n_chunks, chunk, 0)


def _mlstm(proj, conv_qk, i_pre, f_pre, ml_norm):
    b, s, _ = proj.shape
    d = ML_HEAD_DIM
    h = ML_HEADS
    n_chunks = s // ML_CHUNK

    def col_spec(group):
        return pl.BlockSpec((None, s, d), lambda bi, hi: (bi, 0, group * h + hi))

    gate_spec = pl.BlockSpec((None, None, n_chunks, ML_CHUNK), lambda bi, hi: (bi, hi, 0, 0))
    return pl.pallas_call(
        _mlstm_kernel,
        grid=(b, h),
        in_specs=[
            col_spec(0), col_spec(1), col_spec(2), col_spec(3),
            pl.BlockSpec((QK_CONV, d), lambda bi, hi: (0, hi)),
            pl.BlockSpec((QK_CONV, d), lambda bi, hi: (0, h + hi)),
            gate_spec, gate_spec,
            pl.BlockSpec((1, d), lambda bi, hi: (0, hi)),
        ],
        out_specs=pl.BlockSpec((None, s, d), lambda bi, hi: (bi, 0, hi)),
        out_shape=jax.ShapeDtypeStruct((b, s, ML_WIDTH), BF16),
        scratch_shapes=[pltpu.VMEM((d, d), F32), pltpu.VMEM((1, d), F32), pltpu.VMEM((1, 1), F32)],
        compiler_params=_params("parallel", "parallel"),
        name="mlstm",
    )(proj, proj, proj, proj, conv_qk, conv_qk, i_pre, f_pre, ml_norm)


def _sb_kernel(q_ref, k_ref, v_ref, out_ref, *, tq):
    qi = pl.program_id(2)
    q = q_ref[...]
    scale = SB_HEAD_DIM ** -0.5
    row = lax.broadcasted_iota(jnp.int32, (tq, tq), 0)
    col = lax.broadcasted_iota(jnp.int32, (tq, tq), 1)
    strict = col < row
    suffix = (row > col).astype(BF16)

    def block(kb, run, acc, diag):
        start = pl.multiple_of(kb * tq, tq)
        k = k_ref[pl.ds(start, tq), :]
        v = v_ref[pl.ds(start, tq), :]
        z = lax.dot_general(q, k, NT_DIMS, preferred_element_type=F32) * scale
        sp = _softplus(z)
        log_beta = z - sp
        if diag:
            sp = jnp.where(strict, sp, 0.0)
        sp_hi = sp.astype(BF16)
        sp_lo = (sp - sp_hi.astype(F32)).astype(BF16)
        rest = (jnp.dot(sp_hi, suffix, preferred_element_type=F32)
                + jnp.dot(sp_lo, suffix, preferred_element_type=F32))
        w = jnp.exp(log_beta - rest - run)
        if diag:
            w = jnp.where(strict, w, 0.0)
        acc = acc + jnp.dot(w.astype(BF16), v, preferred_element_type=F32)
        run = run + rest[:, 0:1] + sp[:, 0:1]
        return run, acc

    run, acc = block(qi, jnp.zeros((tq, 1), F32), jnp.zeros((tq, SB_HEAD_DIM), F32), True)

    def body(t, carry):
        return block(qi - 1 - t, carry[0], carry[1], False)

    run, acc = lax.fori_loop(0, qi, body, (run, acc))
    out_ref[...] = acc.astype(out_ref.dtype)


def _stick_breaking(proj, col0, *, tq):
    b, s, _ = proj.shape
    d = SB_HEAD_DIM
    h = SB_HEADS
    return pl.pallas_call(
        functools.partial(_sb_kernel, tq=tq),
        grid=(b, h, s // tq),
        in_specs=[
            pl.BlockSpec((None, tq, d), lambda bi, hi, qi: (bi, qi, col0 + hi)),
            pl.BlockSpec((None, s, d), lambda bi, hi, qi: (bi, 0, col0 + h + hi)),
            pl.BlockSpec((None, s, d), lambda bi, hi, qi: (bi, 0, col0 + 2 * h + hi)),
        ],
        out_specs=pl.BlockSpec((None, tq, d), lambda bi, hi, qi: (bi, qi, hi)),
        out_shape=jax.ShapeDtypeStruct((b, s, SB_WIDTH), BF16),
        compiler_params=_params("parallel", "parallel", "arbitrary"),
        name="stick_breaking",
    )(proj, proj, proj)


def _proj2_res_kernel(x_ref, a_ref, b_ref, wa_ref, wb_ref, out_ref):
    out_ref[...] = (x_ref[...]
                    + jnp.dot(a_ref[...], wa_ref[...], preferred_element_type=F32)
                    + jnp.dot(b_ref[...], wb_ref[...], preferred_element_type=F32))


def _proj2_res(x, a, b, wa, wb, *, tm, tn):
    m, n = x.shape
    ka, kb = a.shape[1], b.shape[1]
    return pl.pallas_call(
        _proj2_res_kernel,
        grid=(m // tm, n // tn),
        in_specs=[
            pl.BlockSpec((tm, tn), lambda i, j: (i, j)),
            pl.BlockSpec((tm, ka), lambda i, j: (i, 0)),
            pl.BlockSpec((tm, kb), lambda i, j: (i, 0)),
            pl.BlockSpec((ka, tn), lambda i, j: (0, j)),
            pl.BlockSpec((kb, tn), lambda i, j: (0, j)),
        ],
        out_specs=pl.BlockSpec((tm, tn), lambda i, j: (i, j)),
        out_shape=jax.ShapeDtypeStruct((m, n), F32),
        compiler_params=_params("parallel", "arbitrary"),
        name="out_proj",
    )(x, a, b, wa, wb)


def _proj_res_kernel(x_ref, a_ref, w_ref, out_ref):
    out_ref[...] = x_ref[...] + jnp.dot(a_ref[...], w_ref[...], preferred_element_type=F32)


def _proj_res(x, a, w, *, tm, tn):
    m, n = x.shape
    k = a.shape[1]
    return pl.pallas_call(
        _proj_res_kernel,
        grid=(m // tm, n // tn),
        in_specs=[
            pl.BlockSpec((tm, tn), lambda i, j: (i, j)),
            pl.BlockSpec((tm, k), lambda i, j: (i, 0)),
            pl.BlockSpec((k, tn), lambda i, j: (0, j)),
        ],
        out_specs=pl.BlockSpec((tm, tn), lambda i, j: (i, j)),
        out_shape=jax.ShapeDtypeStruct((m, n), F32),
        compiler_params=_params("parallel", "arbitrary"),
        name="pw2",
    )(x, a, w)


def _swiglu_kernel(x_ref, g_ref, w1_ref, w3_ref, w2_ref, out_ref, xn_ref, acc_ref):
    j = pl.program_id(1)

    @pl.when(j == 0)
    def _():
        xn_ref[...] = _rms_norm(x_ref[...], g_ref[...]).astype(BF16)
        acc_ref[...] = jnp.zeros_like(acc_ref)

    xn = xn_ref[...]
    a = jnp.dot(xn, w1_ref[...], preferred_element_type=F32)
    b = jnp.dot(xn, w3_ref[...], preferred_element_type=F32)
    h = (a * _sigmoid(a) * b).astype(BF16)
    acc_ref[...] += jnp.dot(h, w2_ref[...], preferred_element_type=F32)

    @pl.when(j == pl.num_programs(1) - 1)
    def _():
        out_ref[...] = x_ref[...] + acc_ref[...]


def _swiglu_res(x, g, w1, w3, w2, *, tm, tf):
    m, d = x.shape
    f = w1.shape[1]
    return pl.pallas_call(
        _swiglu_kernel,
        grid=(m // tm, f // tf),
        in_specs=[
            pl.BlockSpec((tm, d), lambda i, j: (i, 0)),
            pl.BlockSpec((1, d), lambda i, j: (0, 0)),
            pl.BlockSpec((d, tf), lambda i, j: (0, j)),
            pl.BlockSpec((d, tf), lambda i, j: (0, j)),
            pl.BlockSpec((tf, d), lambda i, j: (j, 0)),
        ],
        out_specs=pl.BlockSpec((tm, d), lambda i, j: (i, 0)),
        out_shape=jax.ShapeDtypeStruct((m, d), F32),
        scratch_shapes=[pltpu.VMEM((tm, d), BF16), pltpu.VMEM((tm, d), F32)],
        compiler_params=_params("parallel", "arbitrary"),
        name="swiglu",
    )(x, g, w1, w3, w2)


def _pw1_glu_kernel(x_ref, g_ref, wa_ref, wg_ref, out_ref, xn_ref):
    @pl.when(pl.program_id(1) == 0)
    def _():
        xn_ref[...] = _rms_norm(x_ref[...], g_ref[...]).astype(BF16)

    xn = xn_ref[...]
    a = jnp.dot(xn, wa_ref[...], preferred_element_type=F32)
    gate = jnp.dot(xn, wg_ref[...], preferred_element_type=F32)
    out_ref[...] = (a * _sigmoid(gate)).astype(out_ref.dtype)


def _pw1_glu(x, g, wa, wg, *, tm, tn):
    m, d = x.shape
    n = wa.shape[1]
    return pl.pallas_call(
        _pw1_glu_kernel,
        grid=(m // tm, n // tn),
        in_specs=[
            pl.BlockSpec((tm, d), lambda i, j: (i, 0)),
            pl.BlockSpec((1, d), lambda i, j: (0, 0)),
            pl.BlockSpec((d, tn), lambda i, j: (0, j)),
            pl.BlockSpec((d, tn), lambda i, j: (0, j)),
        ],
        out_specs=pl.BlockSpec((tm, tn), lambda i, j: (i, j)),
        out_shape=jax.ShapeDtypeStruct((m, n), BF16),
        scratch_shapes=[pltpu.VMEM((tm, d), BF16)],
        compiler_params=_params("parallel", "arbitrary"),
        name="pw1_glu",
    )(x, g, wa, wg)


CONV_HALO = 32
CONV_ROWS = 64


def _conv_ln_kernel(halo_ref, u_ref, w_ref, g_ref, b_ref, out_ref, xx_ref, y_ref, *, ts):
    ch = u_ref.shape[1]
    first = pl.program_id(1) == 0
    halo = halo_ref[...].astype(F32)
    xx_ref[pl.ds(0, CONV_HALO), :] = jnp.where(first, 0.0, halo)
    xx_ref[pl.ds(CONV_HALO, ts), :] = u_ref[...].astype(F32)

    def lane_block(cb, carry):
        c0 = pl.multiple_of(cb * LANES, LANES)
        w = w_ref[:, pl.ds(c0, LANES)]
        for rb in range(ts // CONV_ROWS):
            acc = jnp.zeros((CONV_ROWS, LANES), F32)
            for j in range(CONV_WIDTH):
                off = rb * CONV_ROWS + CONV_HALO - (CONV_WIDTH - 1) + j
                acc = acc + w[j:j + 1, :] * xx_ref[pl.ds(off, CONV_ROWS), pl.ds(c0, LANES)]
            y_ref[pl.ds(rb * CONV_ROWS, CONV_ROWS), pl.ds(c0, LANES)] = acc
        return carry

    lax.fori_loop(0, ch // LANES, lane_block, 0)

    y = y_ref[...]
    mu = jnp.mean(y, axis=-1, keepdims=True)
    yc = y - mu
    yn = yc * lax.rsqrt(jnp.mean(yc * yc, axis=-1, keepdims=True) + EPS) * g_ref[...] + b_ref[...]
    out_ref[...] = (yn * _sigmoid(yn)).astype(out_ref.dtype)


def _conv_ln(u, w, g, b, *, ts):
    bsz, s, ch = u.shape
    per = ts // CONV_HALO
    return pl.pallas_call(
        functools.partial(_conv_ln_kernel, ts=ts),
        grid=(bsz, s // ts),
        in_specs=[
            pl.BlockSpec((None, CONV_HALO, ch), lambda bi, i: (bi, jnp.maximum(i * per - 1, 0), 0)),
            pl.BlockSpec((None, ts, ch), lambda bi, i: (bi, i, 0)),
            pl.BlockSpec((CONV_WIDTH, ch), lambda bi, i: (0, 0)),
            pl.BlockSpec((1, ch), lambda bi, i: (0, 0)),
            pl.BlockSpec((1, ch), lambda bi, i: (0, 0)),
        ],
        out_specs=pl.BlockSpec((None, ts, ch), lambda bi, i: (bi, i, 0)),
        out_shape=jax.ShapeDtypeStruct((bsz, s, ch), BF16),
        scratch_shapes=[pltpu.VMEM((CONV_HALO + ts, ch), F32), pltpu.VMEM((ts, ch), F32)],
        compiler_params=_params("parallel", "parallel"),
        name="conv_ln",
    )(u, u, w, g, b)


def _router_kernel(x_ref, g_ref, wr_ref, br_ref, xn_ref, idx_ref, gate_ref):
    xn = _rms_norm(x_ref[...], g_ref[...])
    xn_ref[...] = xn
    logits = jnp.dot(xn, wr_ref[...], precision=HIGHEST, preferred_element_type=F32) + br_ref[...]
    lane = lax.broadcasted_iota(jnp.int32, logits.shape, 1)
    logits = jnp.where(lane < N_EXPERTS, logits, -jnp.inf)
    m1 = jnp.max(logits, axis=1, keepdims=True)
    i1 = jnp.min(jnp.where(logits == m1, lane, LANES), axis=1, keepdims=True)
    rest = jnp.where(lane == i1, -jnp.inf, logits)
    m2 = jnp.max(rest, axis=1, keepdims=True)
    i2 = jnp.min(jnp.where(rest == m2, lane, LANES), axis=1, keepdims=True)
    e2 = jnp.exp(m2 - m1)
    g1 = 1.0 / (1.0 + e2)
    g2 = e2 / (1.0 + e2)
    idx_ref[...] = jnp.where(lane == 0, i1, i2)
    gate_ref[...] = jnp.where(lane == 0, g1, g2)


def _router(x, g, wr, br, *, tm):
    m, d = x.shape
    return pl.pallas_call(
        _router_kernel,
        grid=(m // tm,),
        in_specs=[
            pl.BlockSpec((tm, d), lambda i: (i, 0)),
            pl.BlockSpec((1, d), lambda i: (0, 0)),
            pl.BlockSpec((d, LANES), lambda i: (0, 0)),
            pl.BlockSpec((1, LANES), lambda i: (0, 0)),
        ],
        out_specs=[
            pl.BlockSpec((tm, d), lambda i: (i, 0)),
            pl.BlockSpec((tm, LANES), lambda i: (i, 0)),
            pl.BlockSpec((tm, LANES), lambda i: (i, 0)),
        ],
        out_shape=[
            jax.ShapeDtypeStruct((m, d), F32),
            jax.ShapeDtypeStruct((m, LANES), jnp.int32),
            jax.ShapeDtypeStruct((m, LANES), F32),
        ],
        compiler_params=_params("parallel"),
        name="router",
    )(x, g, wr, br)


def _row_copy(src_hbm, src_row, dst_ref, dst_row, sem):
    return pltpu.make_async_copy(src_hbm.at[pl.ds(src_row, 1)], dst_ref.at[pl.ds(dst_row, 1)], sem)


def _gather_kernel(tok_ref, x_hbm, out_ref, sem, *, tg):
    def issue(r, carry):
        _row_copy(x_hbm, tok_ref[r], out_ref, r, sem).start()
        return carry

    lax.fori_loop(0, tg, issue, 0)

    def drain(r, carry):
        _row_copy(x_hbm, 0, out_ref, r, sem).wait()
        return carry

    lax.fori_loop(0, tg, drain, 0)


def _gather_rows(row_tok, x, *, tg):
    n_rows = row_tok.shape[0]
    d = x.shape[1]
    return pl.pallas_call(
        functools.partial(_gather_kernel, tg=tg),
        grid=(n_rows // tg,),
        in_specs=[
            pl.BlockSpec((tg,), lambda i: (i,), memory_space=pltpu.SMEM),
            pl.BlockSpec(memory_space=pl.ANY),
        ],
        out_specs=pl.BlockSpec((tg, d), lambda i: (i, 0)),
        out_shape=jax.ShapeDtypeStruct((n_rows, d), x.dtype),
        scratch_shapes=[pltpu.SemaphoreType.DMA(())],
        compiler_params=_params("arbitrary"),
        name="moe_gather",
    )(row_tok, x)


def _moe_kernel(be_ref, nu_ref, x_ref, w1_ref, w3_ref, w2_ref, out_ref, xb_ref, acc_ref):
    i = pl.program_id(0)
    j = pl.program_id(1)
    used = i < nu_ref[0]

    @pl.when(jnp.logical_and(used, j == 0))
    def _():
        xb_ref[...] = x_ref[...].astype(BF16)
        acc_ref[...] = jnp.zeros_like(acc_ref)

    @pl.when(used)
    def _():
        xb = xb_ref[...]
        a = jnp.dot(xb, w1_ref[...], preferred_element_type=F32)
        b = jnp.dot(xb, w3_ref[...], preferred_element_type=F32)
        h = (a * _sigmoid(a) * b).astype(BF16)
        acc_ref[...] += jnp.dot(h, w2_ref[...], preferred_element_type=F32)

    @pl.when(j == pl.num_programs(1) - 1)
    def _():
        out_ref[...] = jnp.where(used, acc_ref[...], 0.0)


def _moe_experts(block_e, n_used, x_rows, w1, w3, w2, *, tm, tf):
    n_rows, d = x_rows.shape
    f = w1.shape[2]
    nf = f // tf

    def ff(i, j, nu):
        return jnp.where(i < nu[0], j, nf - 1)

    return pl.pallas_call(
        _moe_kernel,
        grid_spec=pltpu.PrefetchScalarGridSpec(
            num_scalar_prefetch=2,
            grid=(n_rows // tm, nf),
            in_specs=[
                pl.BlockSpec((tm, d), lambda i, j, be, nu: (jnp.minimum(i, nu[0] - 1), 0)),
                pl.BlockSpec((None, d, tf), lambda i, j, be, nu: (be[i], 0, ff(i, j, nu))),
                pl.BlockSpec((None, d, tf), lambda i, j, be, nu: (be[i], 0, ff(i, j, nu))),
                pl.BlockSpec((None, tf, d), lambda i, j, be, nu: (be[i], ff(i, j, nu), 0)),
            ],
            out_specs=pl.BlockSpec((tm, d), lambda i, j, be, nu: (i, 0)),
            scratch_shapes=[pltpu.VMEM((tm, d), BF16), pltpu.VMEM((tm, d), F32)],
        ),
        out_shape=jax.ShapeDtypeStruct((n_rows, d), F32),
        compiler_params=_params("arbitrary", "arbitrary"),
        name="moe_experts",
    )(block_e, n_used, x_rows, w1, w3, w2)


def _combine_kernel(dest_ref, y_hbm, h_ref, gate_ref, fn_ref, out_ref, buf_ref, sem, *, tc):
    def issue(r, carry):
        for k in range(TOP_K):
            _row_copy(y_hbm, dest_ref[TOP_K * r + k], buf_ref.at[k], r, sem).start()
        return carry

    lax.fori_loop(0, tc, issue, 0)

    def drain(r, carry):
        for k in range(TOP_K):
            _row_copy(y_hbm, 0, buf_ref.at[k], r, sem).wait()
        return carry

    lax.fori_loop(0, tc, drain, 0)

    gates = gate_ref[...]
    y = h_ref[...] + gates[:, 0:1] * buf_ref[0] + gates[:, 1:2] * buf_ref[1]
    out_ref[...] = _rms_norm(y, fn_ref[...])


def _combine(dest, y_rows, h, gates, fn, *, tc):
    m, d = h.shape
    return pl.pallas_call(
        functools.partial(_combine_kernel, tc=tc),
        grid=(m // tc,),
        in_specs=[
            pl.BlockSpec((TOP_K * tc,), lambda i: (i,), memory_space=pltpu.SMEM),
            pl.BlockSpec(memory_space=pl.ANY),
            pl.BlockSpec((tc, d), lambda i: (i, 0)),
            pl.BlockSpec((tc, LANES), lambda i: (i, 0)),
            pl.BlockSpec((1, d), lambda i: (0, 0)),
        ],
        out_specs=pl.BlockSpec((tc, d), lambda i: (i, 0)),
        out_shape=jax.ShapeDtypeStruct((m, d), F32),
        scratch_shapes=[pltpu.VMEM((TOP_K, tc, d), F32), pltpu.SemaphoreType.DMA(())],
        compiler_params=_params("arbitrary"),
        name="moe_combine",
    )(dest, y_rows, h, gates, fn)


def _dispatch_plan(top_idx, n_tok, tm):
    flat_e = top_idx.reshape(-1)
    n_asg = flat_e.shape[0]
    onehot = (flat_e[:, None] == jnp.arange(N_EXPERTS, dtype=jnp.int32)[None, :]).astype(jnp.int32)
    csum = jnp.cumsum(onehot, axis=0)
    rank = jnp.sum(onehot * csum, axis=1) - 1
    counts = csum[-1]
    padded = ((counts + tm - 1) // tm) * tm
    pad_end = jnp.cumsum(padded)
    pad_start = pad_end - padded
    dest = (pad_start[flat_e] + rank).astype(jnp.int32)
    n_blocks = n_asg // tm + N_EXPERTS
    n_used = (pad_end[-1] // tm).astype(jnp.int32)
    flat_tok = jnp.arange(n_asg, dtype=jnp.int32) // TOP_K
    row_tok = jnp.zeros((n_blocks * tm,), jnp.int32).at[dest].set(flat_tok)
    blk = jnp.minimum(jnp.arange(n_blocks, dtype=jnp.int32), n_used - 1) * tm
    block_e = jnp.minimum(jnp.searchsorted(pad_end, blk, side="right"), N_EXPERTS - 1).astype(jnp.int32)
    return dest, row_tok, block_e, n_used.reshape(1)


def _pad_lanes(a):
    return jnp.pad(a, [(0, 0)] * (a.ndim - 1) + [(0, LANES - a.shape[-1])])


def kernel(x, ev_norm_mix, ev_w_in, ev_conv_qk, ev_b_gates, ev_ml_norm, ev_w_out, ev_norm_ffn,
           ev_w1, ev_w3, ev_w2, od_norm_mix, od_pw1, od_dw, od_ln_g, od_ln_b, od_pw2,
           od_norm_ffn, od_router, od_router_b, od_moe_w1, od_moe_w3, od_moe_w2, final_norm):
    bsz, s, d = x.shape
    m = bsz * s
    h = x.reshape(m, d)

    w_in = ev_w_in[0]
    gate0 = 4 * ML_WIDTH
    gate1 = gate0 + 2 * ML_HEADS
    w_main = jnp.concatenate([w_in[:, :gate0], w_in[:, gate1:]], axis=1).astype(BF16)
    w_gates = _pad_lanes(w_in[:, gate0:gate1])
    b_gates = _pad_lanes(ev_b_gates[0][None, :])
    proj, gates = _in_proj(h, ev_norm_mix[0][None, :], w_main, w_gates, b_gates, tm=1024, tn=1024)
    proj = proj.reshape(bsz, s, -1)
    n_chunks = s // ML_CHUNK
    gates = gates[:, :2 * ML_HEADS].reshape(bsz, s, 2 * ML_HEADS).transpose(0, 2, 1)
    i_pre = gates[:, :ML_HEADS].reshape(bsz, ML_HEADS, n_chunks, ML_CHUNK)
    f_pre = gates[:, ML_HEADS:].reshape(bsz, ML_HEADS, n_chunks, ML_CHUNK)
    h_ml = _mlstm(proj, ev_conv_qk[0], i_pre, f_pre, ev_ml_norm[0][None, :])
    h_sb = _stick_breaking(proj, gate0 // SB_HEAD_DIM, tq=256)
    w_out = ev_w_out[0].astype(BF16)
    h = _proj2_res(h, h_ml.reshape(m, ML_WIDTH), h_sb.reshape(m, SB_WIDTH),
                   w_out[:ML_WIDTH], w_out[ML_WIDTH:], tm=1024, tn=1024)

    h = _swiglu_res(h, ev_norm_ffn[0][None, :], ev_w1[0].astype(BF16), ev_w3[0].astype(BF16),
                    ev_w2[0].astype(BF16), tm=512, tf=512)

    pw1 = od_pw1[0].astype(BF16)
    cw = pw1.shape[1] // 2
    u = _pw1_glu(h, od_norm_mix[0][None, :], pw1[:, :cw], pw1[:, cw:], tm=1024, tn=512)
    u = _conv_ln(u.reshape(bsz, s, cw), od_dw[0], od_ln_g[0][None, :], od_ln_b[0][None, :], ts=256)
    h = _proj_res(h, u.reshape(m, cw), od_pw2[0].astype(BF16), tm=1024, tn=1024)

    tm_e = 512
    xn, top_idx, top_gate = _router(h, od_norm_ffn[0][None, :], _pad_lanes(od_router[0]),
                                    _pad_lanes(od_router_b[0][None, :]), tm=512)
    dest, row_tok, block_e, n_used = _dispatch_plan(top_idx[:, :TOP_K], m, tm_e)
    x_rows = _gather_rows(row_tok, xn, tg=256)
    y_rows = _moe_experts(block_e, n_used, x_rows, od_moe_w1[0].astype(BF16),
                          od_moe_w3[0].astype(BF16), od_moe_w2[0].astype(BF16), tm=tm_e, tf=512)
    out = _combine(dest, y_rows, h, top_gate, final_norm[None, :], tc=256)
    return out.reshape(bsz, s, d)
```

```python
import functools

import jax
import jax.numpy as jnp
from jax import lax
from jax.experimental import pallas as pl
from jax.experimental.pallas import tpu as pltpu

ML_HEADS = 4
ML_HEAD_DIM = 256
ML_WIDTH = ML_HEADS * ML_HEAD_DIM
SB_HEADS = 8
SB_HEAD_DIM = 128
SB_WIDTH = SB_HEADS * SB_HEAD_DIM
QK_CONV = 4
ML_CHUNK = 128
CONV_WIDTH = 31
N_EXPERTS = 8
TOP_K = 2
EPS = 1e-6

BF16 = jnp.bfloat16
F32 = jnp.float32
HIGHEST = lax.Precision.HIGHEST

LANES = 128
F32_SUBLANES = 8
BF16_SUBLANES = 16
V7X_VMEM_LIMIT = 56 * 1024 * 1024

NT_DIMS = (((1,), (1,)), ((), ()))
TN_DIMS = (((0,), (0,)), ((), ()))


def _params(*semantics):
    return pltpu.CompilerParams(dimension_semantics=semantics, vmem_limit_bytes=V7X_VMEM_LIMIT)


def _rms_norm(x, g):
    return x * lax.rsqrt(jnp.mean(x * x, axis=-1, keepdims=True) + EPS) * g


def _sigmoid(x):
    return 1.0 / (1.0 + jnp.exp(-x))


def _softplus(z):
    return jnp.maximum(z, 0.0) + jnp.log(1.0 + jnp.exp(-jnp.abs(z)))


def _in_proj_kernel(x_ref, g_ref, w_ref, wg_ref, bg_ref, proj_ref, gates_ref, xn_ref):
    @pl.when(pl.program_id(1) == 0)
    def _():
        xn = _rms_norm(x_ref[...], g_ref[...])
        xn_ref[...] = xn.astype(BF16)
        gates_ref[...] = jnp.dot(xn, wg_ref[...], precision=HIGHEST,
                                 preferred_element_type=F32) + bg_ref[...]

    proj_ref[...] = jnp.dot(xn_ref[...], w_ref[...], preferred_element_type=F32).astype(BF16)


def _in_proj(x, g, w, wg, bg, *, tm, tn):
    m, d = x.shape
    n = w.shape[1]
    return pl.pallas_call(
        _in_proj_kernel,
        grid=(m // tm, n // tn),
        in_specs=[
            pl.BlockSpec((tm, d), lambda i, j: (i, 0)),
            pl.BlockSpec((1, d), lambda i, j: (0, 0)),
            pl.BlockSpec((d, tn), lambda i, j: (0, j)),
            pl.BlockSpec((d, LANES), lambda i, j: (0, 0)),
            pl.BlockSpec((1, LANES), lambda i, j: (0, 0)),
        ],
        out_specs=[
            pl.BlockSpec((tm, tn), lambda i, j: (i, j)),
            pl.BlockSpec((tm, LANES), lambda i, j: (i, 0)),
        ],
        out_shape=[jax.ShapeDtypeStruct((m, n), BF16), jax.ShapeDtypeStruct((m, LANES), F32)],
        scratch_shapes=[pltpu.VMEM((tm, d), BF16)],
        compiler_params=_params("parallel", "arbitrary"),
        name="in_proj",
    )(x, g, w, wg, bg)


def _mlstm_kernel(q_ref, k_ref, v_ref, o_ref, cq_ref, ck_ref, i_ref, f_ref, nrm_ref,
                  out_ref, c_state, n_state, m_state, *, hps):
    L = ML_CHUNK
    d = ML_HEAD_DIM
    n_chunks = q_ref.shape[0] // L
    halo = BF16_SUBLANES

    c_state[...] = jnp.zeros_like(c_state)
    n_state[...] = jnp.zeros_like(n_state)
    m_state[...] = jnp.zeros_like(m_state)

    row = lax.broadcasted_iota(jnp.int32, (L, L), 0)
    col = lax.broadcasted_iota(jnp.int32, (L, L), 1)
    causal = col <= row
    eye = col == row

    def conv_swish(x_ref, w_ref, c, lanes):
        start = pl.multiple_of(c * L, L)
        cur = x_ref[pl.ds(start, L), lanes].astype(F32)
        prev_start = pl.multiple_of(jnp.maximum(start - halo, 0), halo)
        prev = x_ref[pl.ds(prev_start, halo), lanes].astype(F32)
        prev = jnp.where(c > 0, prev, 0.0)
        xx = jnp.concatenate([prev, cur], axis=0)
        w = w_ref[:, lanes]
        y = jnp.zeros((L, d), F32)
        for j in range(QK_CONV):
            off = halo - (QK_CONV - 1) + j
            y = y + w[j:j + 1, :] * xx[off:off + L, :]
        return y * _sigmoid(y)

    def to_col(r):
        return jnp.sum(jnp.where(eye, r, 0.0), axis=1, keepdims=True)

    def head_chunk(hh, c):
        lanes = slice(hh * d, (hh + 1) * d)
        start = pl.multiple_of(c * L, L)
        q = conv_swish(q_ref, cq_ref, c, lanes)
        k = conv_swish(k_ref, ck_ref, c, lanes) * (d ** -0.5)
        v = v_ref[pl.ds(start, L), lanes]
        li_row = i_ref[hh, pl.ds(c, 1), :]
        f_row = f_ref[hh, pl.ds(c, 1), :]
        lf_row = -_softplus(-f_row)
        lf_col = to_col(lf_row)
        li_col = to_col(li_row)
        bl_col = jnp.sum(jnp.where(causal, lf_row, 0.0), axis=1, keepdims=True)
        bl_row = jnp.sum(jnp.where(col >= row, lf_col, 0.0), axis=0, keepdims=True)
        g = jnp.sum(lf_row, axis=1, keepdims=True)

        c_prev = c_state[hh]
        n_prev = n_state[hh]
        m_prev = m_state[hh]

        d_log = jnp.where(causal, bl_col - bl_row + li_row, -jnp.inf)
        inter = bl_col + m_prev
        m_t = jnp.maximum(inter, jnp.max(d_log, axis=1, keepdims=True))
        w_intra = jnp.exp(d_log - m_t)
        w_inter = jnp.exp(inter - m_t)
        q_b = q.astype(BF16)
        k_b = k.astype(BF16)
        s_qk = lax.dot_general(q_b, k_b, NT_DIMS, preferred_element_type=F32) * w_intra
        num = (w_inter * jnp.dot(q_b, c_prev.astype(BF16), preferred_element_type=F32)
               + jnp.dot(s_qk.astype(BF16), v, preferred_element_type=F32))
        qn = (w_inter * jnp.sum(q * n_prev, axis=1, keepdims=True)
              + jnp.sum(s_qk, axis=1, keepdims=True))
        denom = jnp.maximum(jnp.abs(qn), jnp.exp(-m_t))
        h = num / denom
        h = h * lax.rsqrt(jnp.mean(h * h, axis=1, keepdims=True) + EPS)
        h = h * nrm_ref[:, lanes] * _sigmoid(o_ref[pl.ds(start, L), lanes].astype(F32))
        out_ref[pl.ds(start, L), lanes] = h.astype(out_ref.dtype)

        a_col = g - bl_col + li_col
        m_loc = jnp.max(a_col, axis=0, keepdims=True)
        kw = k * jnp.exp(a_col - m_loc)
        c_loc = lax.dot_general(kw.astype(BF16), v, TN_DIMS, preferred_element_type=F32)
        n_loc = jnp.sum(kw, axis=0, keepdims=True)
        m_new = jnp.maximum(g + m_prev, m_loc)
        s_prev = jnp.exp(g + m_prev - m_new)
        s_loc = jnp.exp(m_loc - m_new)
        c_state[hh] = s_prev * c_prev + s_loc * c_loc
        n_state[hh] = s_prev * n_prev + s_loc * n_loc
        m_state[hh] = m_new

    def chunk(c, carry):
        for hh in range(hps):
            head_chunk(hh, c)
        return carry

    lax.fori_loop(0, n_chunks, chunk, 0)


def _mlstm(proj, conv_qk, i_pre, f_pre, ml_norm, *, hps):
    b, s, _ = proj.shape
    d = ML_HEAD_DIM
    w = hps * d
    groups = ML_HEADS // hps
    n_chunks = s // ML_CHUNK

    def col_spec(part):
        return pl.BlockSpec((None, s, w), lambda bi, hi: (bi, 0, part * groups + hi))

    gate_spec = pl.BlockSpec((None, hps, n_chunks, ML_CHUNK), lambda bi, hi: (bi, hi, 0, 0))
    return pl.pallas_call(
        functools.partial(_mlstm_kernel, hps=hps),
        grid=(b, groups),
        in_specs=[
            col_spec(0), col_spec(1), col_spec(2), col_spec(3),
            pl.BlockSpec((QK_CONV, w), lambda bi, hi: (0, hi)),
            pl.BlockSpec((QK_CONV, w), lambda bi, hi: (0, groups + hi)),
            gate_spec, gate_spec,
            pl.BlockSpec((1, w), lambda bi, hi: (0, hi)),
        ],
        out_specs=pl.BlockSpec((None, s, w), lambda bi, hi: (bi, 0, hi)),
        out_shape=jax.ShapeDtypeStruct((b, s, ML_WIDTH), BF16),
        scratch_shapes=[pltpu.VMEM((hps, d, d), F32), pltpu.VMEM((hps, 1, d), F32),
                        pltpu.VMEM((hps, 1, 1), F32)],
        compiler_params=_params("parallel", "parallel"),
        name="mlstm",
    )(proj, proj, proj, proj, conv_qk, conv_qk, i_pre, f_pre, ml_norm)


def _sb_kernel(q_ref, k_ref, v_ref, out_ref, *, tq, hps):
    qi = pl.program_id(2)
    d = SB_HEAD_DIM
    scale = d ** -0.5
    row = lax.broadcasted_iota(jnp.int32, (tq, tq), 0)
    col = lax.broadcasted_iota(jnp.int32, (tq, tq), 1)
    strict = col < row
    suffix = (row > col).astype(BF16)

    def block(kb, carry, diag):
        start = pl.multiple_of(kb * tq, tq)
        heads = range(hps)
        lanes = [slice(hh * d, (hh + 1) * d) for hh in heads]
        z = [lax.dot_general(q_ref[:, lanes[hh]], k_ref[pl.ds(start, tq), lanes[hh]], NT_DIMS,
                             preferred_element_type=F32) * scale for hh in heads]
        sp, log_beta, sp_hi, sp_lo = [], [], [], []
        for hh in heads:
            s = _softplus(z[hh])
            log_beta.append(z[hh] - s)
            if diag:
                s = jnp.where(strict, s, 0.0)
            hi = s.astype(BF16)
            sp.append(s)
            sp_hi.append(hi)
            sp_lo.append((s - hi.astype(F32)).astype(BF16))
        rest = [jnp.dot(sp_hi[hh], suffix, preferred_element_type=F32)
                + jnp.dot(sp_lo[hh], suffix, preferred_element_type=F32) for hh in heads]
        w = []
        for hh in heads:
            wh = jnp.exp(log_beta[hh] - rest[hh] - carry[hh][0])
            if diag:
                wh = jnp.where(strict, wh, 0.0)
            w.append(wh.astype(BF16))
        out = []
        for hh in heads:
            acc = carry[hh][1] + jnp.dot(w[hh], v_ref[pl.ds(start, tq), lanes[hh]],
                                         preferred_element_type=F32)
            run = carry[hh][0] + rest[hh][:, 0:1] + sp[hh][:, 0:1]
            out.append((run, acc))
        return tuple(out)

    init = tuple((jnp.zeros((tq, 1), F32), jnp.zeros((tq, d), F32)) for _ in range(hps))
    carry = block(qi, init, True)
    carry = lax.fori_loop(0, qi, lambda t, c: block(qi - 1 - t, c, False), carry)
    for hh in range(hps):
        out_ref[:, hh * d:(hh + 1) * d] = carry[hh][1].astype(out_ref.dtype)


def _stick_breaking(proj, col0, *, tq, hps):
    b, s, _ = proj.shape
    w = hps * SB_HEAD_DIM
    groups = SB_HEADS // hps
    c0 = col0 // w
    return pl.pallas_call(
        functools.partial(_sb_kernel, tq=tq, hps=hps),
        grid=(b, groups, s // tq),
        in_specs=[
            pl.BlockSpec((None, tq, w), lambda bi, hi, qi: (bi, qi, c0 + hi)),
            pl.BlockSpec((None, s, w), lambda bi, hi, qi: (bi, 0, c0 + groups + hi)),
            pl.BlockSpec((None, s, w), lambda bi, hi, qi: (bi, 0, c0 + 2 * groups + hi)),
        ],
        out_specs=pl.BlockSpec((None, tq, w), lambda bi, hi, qi: (bi, qi, hi)),
        out_shape=jax.ShapeDtypeStruct((b, s, SB_WIDTH), BF16),
        compiler_params=_params("parallel", "parallel", "arbitrary"),
        name="stick_breaking",
    )(proj, proj, proj)


def _proj2_res_kernel(x_ref, a_ref, b_ref, wa_ref, wb_ref, out_ref):
    out_ref[...] = (x_ref[...]
                    + jnp.dot(a_ref[...], wa_ref[...], preferred_element_type=F32)
                    + jnp.dot(b_ref[...], wb_ref[...], preferred_element_type=F32))


def _proj2_res(x, a, b, wa, wb, *, tm, tn):
    m, n = x.shape
    ka, kb = a.shape[1], b.shape[1]
    return pl.pallas_call(
        _proj2_res_kernel,
        grid=(m // tm, n // tn),
        in_specs=[
            pl.BlockSpec((tm, tn), lambda i, j: (i, j)),
            pl.BlockSpec((tm, ka), lambda i, j: (i, 0)),
            pl.BlockSpec((tm, kb), lambda i, j: (i, 0)),
            pl.BlockSpec((ka, tn), lambda i, j: (0, j)),
            pl.BlockSpec((kb, tn), lambda i, j: (0, j)),
        ],
        out_specs=pl.BlockSpec((tm, tn), lambda i, j: (i, j)),
        out_shape=jax.ShapeDtypeStruct((m, n), F32),
        compiler_params=_params("parallel", "arbitrary"),
        name="out_proj",
    )(x, a, b, wa, wb)


def _proj_res_kernel(x_ref, a_ref, w_ref, out_ref):
    out_ref[...] = x_ref[...] + jnp.dot(a_ref[...], w_ref[...], preferred_element_type=F32)


def _proj_res(x, a, w, *, tm, tn):
    m, n = x.shape
    k = a.shape[1]
    return pl.pallas_call(
        _proj_res_kernel,
        grid=(m // tm, n // tn),
        in_specs=[
            pl.BlockSpec((tm, tn), lambda i, j: (i, j)),
            pl.BlockSpec((tm, k), lambda i, j: (i, 0)),
            pl.BlockSpec((k, tn), lambda i, j: (0, j)),
        ],
        out_specs=pl.BlockSpec((tm, tn), lambda i, j: (i, j)),
        out_shape=jax.ShapeDtypeStruct((m, n), F32),
        compiler_params=_params("parallel", "arbitrary"),
        name="pw2",
    )(x, a, w)


def _swiglu_kernel(x_ref, g_ref, w1_ref, w3_ref, w2_ref, out_ref, xn_ref, acc_ref):
    j = pl.program_id(1)

    @pl.when(j == 0)
    def _():
        xn_ref[...] = _rms_norm(x_ref[...], g_ref[...]).astype(BF16)
        acc_ref[...] = jnp.zeros_like(acc_ref)

    xn = xn_ref[...]
    a = jnp.dot(xn, w1_ref[...], preferred_element_type=F32)
    b = jnp.dot(xn, w3_ref[...], preferred_element_type=F32)
    h = (a * _sigmoid(a) * b).astype(BF16)
    acc_ref[...] += jnp.dot(h, w2_ref[...], preferred_element_type=F32)

    @pl.when(j == pl.num_programs(1) - 1)
    def _():
        out_ref[...] = x_ref[...] + acc_ref[...]


def _swiglu_res(x, g, w1, w3, w2, *, tm, tf):
    m, d = x.shape
    f = w1.shape[1]
    return pl.pallas_call(
        _swiglu_kernel,
        grid=(m // tm, f // tf),
        in_specs=[
            pl.BlockSpec((tm, d), lambda i, j: (i, 0)),
            pl.BlockSpec((1, d), lambda i, j: (0, 0)),
            pl.BlockSpec((d, tf), lambda i, j: (0, j)),
            pl.BlockSpec((d, tf), lambda i, j: (0, j)),
            pl.BlockSpec((tf, d), lambda i, j: (j, 0)),
        ],
        out_specs=pl.BlockSpec((tm, d), lambda i, j: (i, 0)),
        out_shape=jax.ShapeDtypeStruct((m, d), F32),
        scratch_shapes=[pltpu.VMEM((tm, d), BF16), pltpu.VMEM((tm, d), F32)],
        compiler_params=_params("parallel", "arbitrary"),
        name="swiglu",
    )(x, g, w1, w3, w2)


def _pw1_glu_kernel(x_ref, g_ref, wa_ref, wg_ref, out_ref, xn_ref):
    @pl.when(pl.program_id(1) == 0)
    def _():
        xn_ref[...] = _rms_norm(x_ref[...], g_ref[...]).astype(BF16)

    xn = xn_ref[...]
    a = jnp.dot(xn, wa_ref[...], preferred_element_type=F32)
    gate = jnp.dot(xn, wg_ref[...], preferred_element_type=F32)
    out_ref[...] = (a * _sigmoid(gate)).astype(out_ref.dtype)


def _pw1_glu(x, g, wa, wg, *, tm, tn):
    m, d = x.shape
    n = wa.shape[1]
    return pl.pallas_call(
        _pw1_glu_kernel,
        grid=(m // tm, n // tn),
        in_specs=[
            pl.BlockSpec((tm, d), lambda i, j: (i, 0)),
            pl.BlockSpec((1, d), lambda i, j: (0, 0)),
            pl.BlockSpec((d, tn), lambda i, j: (0, j)),
            pl.BlockSpec((d, tn), lambda i, j: (0, j)),
        ],
        out_specs=pl.BlockSpec((tm, tn), lambda i, j: (i, j)),
        out_shape=jax.ShapeDtypeStruct((m, n), BF16),
        scratch_shapes=[pltpu.VMEM((tm, d), BF16)],
        compiler_params=_params("parallel", "arbitrary"),
        name="pw1_glu",
    )(x, g, wa, wg)


CONV_HALO = 32
CONV_ROWS = 64


def _conv_ln_kernel(halo_ref, u_ref, w_ref, g_ref, b_ref, out_ref, xx_ref, y_ref, *, ts):
    ch = u_ref.shape[1]
    first = pl.program_id(1) == 0
    halo = halo_ref[...].astype(F32)
    xx_ref[pl.ds(0, CONV_HALO), :] = jnp.where(first, 0.0, halo)
    xx_ref[pl.ds(CONV_HALO, ts), :] = u_ref[...].astype(F32)

    span = CONV_ROWS + CONV_HALO
    first_tap = CONV_HALO - (CONV_WIDTH - 1)

    def lane_block(cb, carry):
        c0 = pl.multiple_of(cb * LANES, LANES)
        w = w_ref[:, pl.ds(c0, LANES)]
        for rb in range(ts // CONV_ROWS):
            window = xx_ref[pl.ds(rb * CONV_ROWS, span), pl.ds(c0, LANES)]
            acc = jnp.zeros((CONV_ROWS, LANES), F32)
            for r in range(F32_SUBLANES):
                shifted = window if r == 0 else pltpu.roll(window, span - r, axis=0)
                for off in range(r, CONV_HALO + 1, F32_SUBLANES):
                    j = off - first_tap
                    if 0 <= j < CONV_WIDTH:
                        acc = acc + w[j:j + 1, :] * shifted[off - r:off - r + CONV_ROWS, :]
            y_ref[pl.ds(rb * CONV_ROWS, CONV_ROWS), pl.ds(c0, LANES)] = acc
        return carry

    lax.fori_loop(0, ch // LANES, lane_block, 0)

    y = y_ref[...]
    mu = jnp.mean(y, axis=-1, keepdims=True)
    yc = y - mu
    yn = yc * lax.rsqrt(jnp.mean(yc * yc, axis=-1, keepdims=True) + EPS) * g_ref[...] + b_ref[...]
    out_ref[...] = (yn * _sigmoid(yn)).astype(out_ref.dtype)


def _conv_ln(u, w, g, b, *, ts):
    bsz, s, ch = u.shape
    per = ts // CONV_HALO
    return pl.pallas_call(
        functools.partial(_conv_ln_kernel, ts=ts),
        grid=(bsz, s // ts),
        in_specs=[
            pl.BlockSpec((None, CONV_HALO, ch), lambda bi, i: (bi, jnp.maximum(i * per - 1, 0), 0)),
            pl.BlockSpec((None, ts, ch), lambda bi, i: (bi, i, 0)),
            pl.BlockSpec((CONV_WIDTH, ch), lambda bi, i: (0, 0)),
            pl.BlockSpec((1, ch), lambda bi, i: (0, 0)),
            pl.BlockSpec((1, ch), lambda bi, i: (0, 0)),
        ],
        out_specs=pl.BlockSpec((None, ts, ch), lambda bi, i: (bi, i, 0)),
        out_shape=jax.ShapeDtypeStruct((bsz, s, ch), BF16),
        scratch_shapes=[pltpu.VMEM((CONV_HALO + ts, ch), F32), pltpu.VMEM((ts, ch), F32)],
        compiler_params=_params("parallel", "parallel"),
        name="conv_ln",
    )(u, u, w, g, b)


def _router_kernel(x_ref, g_ref, wr_ref, br_ref, xn_ref, idx_ref, gate_ref):
    xn = _rms_norm(x_ref[...], g_ref[...])
    xn_ref[...] = xn
    logits = jnp.dot(xn, wr_ref[...], precision=HIGHEST, preferred_element_type=F32) + br_ref[...]
    lane = lax.broadcasted_iota(jnp.int32, logits.shape, 1)
    logits = jnp.where(lane < N_EXPERTS, logits, -jnp.inf)
    m1 = jnp.max(logits, axis=1, keepdims=True)
    i1 = jnp.min(jnp.where(logits == m1, lane, LANES), axis=1, keepdims=True)
    rest = jnp.where(lane == i1, -jnp.inf, logits)
    m2 = jnp.max(rest, axis=1, keepdims=True)
    i2 = jnp.min(jnp.where(rest == m2, lane, LANES), axis=1, keepdims=True)
    e2 = jnp.exp(m2 - m1)
    g1 = 1.0 / (1.0 + e2)
    g2 = e2 / (1.0 + e2)
    idx_ref[...] = jnp.where(lane == 0, i1, i2)
    gate_ref[...] = jnp.where(lane == 0, g1, g2)


def _router(x, g, wr, br, *, tm):
    m, d = x.shape
    return pl.pallas_call(
        _router_kernel,
        grid=(m // tm,),
        in_specs=[
            pl.BlockSpec((tm, d), lambda i: (i, 0)),
            pl.BlockSpec((1, d), lambda i: (0, 0)),
            pl.BlockSpec((d, LANES), lambda i: (0, 0)),
            pl.BlockSpec((1, LANES), lambda i: (0, 0)),
        ],
        out_specs=[
            pl.BlockSpec((tm, d), lambda i: (i, 0)),
            pl.BlockSpec((tm, LANES), lambda i: (i, 0)),
            pl.BlockSpec((tm, LANES), lambda i: (i, 0)),
        ],
        out_shape=[
            jax.ShapeDtypeStruct((m, d), F32),
            jax.ShapeDtypeStruct((m, LANES), jnp.int32),
            jax.ShapeDtypeStruct((m, LANES), F32),
        ],
        compiler_params=_params("parallel"),
        name="router",
    )(x, g, wr, br)


def _row_copy(src_hbm, src_row, dst_ref, dst_row, sem):
    return pltpu.make_async_copy(src_hbm.at[pl.ds(src_row, 1)], dst_ref.at[pl.ds(dst_row, 1)], sem)


def _block_wait(src_hbm, dst_ref, sem):
    pltpu.make_async_copy(src_hbm.at[pl.ds(0, dst_ref.shape[0])], dst_ref, sem).wait()


MOE_GATHER_ROWS = 64


def _moe_kernel(be_ref, nu_ref, tok_ref, x_hbm, w1_ref, w3_ref, w2_ref, out_ref,
                xg_ref, xb_ref, acc_ref, sem, *, tm):
    i = pl.program_id(0)
    j = pl.program_id(1)
    n_used = nu_ref[0]
    used = i < n_used
    slot = i % 2

    def gather(block, dst_slot, r0, n):
        for r in range(n):
            row = r0 + r
            _row_copy(x_hbm, tok_ref[block * tm + row], xg_ref.at[dst_slot], row,
                      sem.at[dst_slot]).start()

    @pl.when(jnp.logical_and(i == 0, j == 0))
    def _():
        def body(c, carry):
            gather(0, 0, c * MOE_GATHER_ROWS, MOE_GATHER_ROWS)
            return carry
        lax.fori_loop(0, tm // MOE_GATHER_ROWS, body, 0)

    @pl.when(jnp.logical_and(i + 1 < n_used, j < tm // MOE_GATHER_ROWS))
    def _():
        gather(i + 1, 1 - slot, j * MOE_GATHER_ROWS, MOE_GATHER_ROWS)

    @pl.when(jnp.logical_and(used, j == 0))
    def _():
        _block_wait(x_hbm, xg_ref.at[slot], sem.at[slot])
        xb_ref[...] = xg_ref[slot].astype(BF16)
        acc_ref[...] = jnp.zeros_like(acc_ref)

    @pl.when(used)
    def _():
        xb = xb_ref[...]
        a = jnp.dot(xb, w1_ref[...], preferred_element_type=F32)
        b = jnp.dot(xb, w3_ref[...], preferred_element_type=F32)
        h = (a * _sigmoid(a) * b).astype(BF16)
        acc_ref[...] += jnp.dot(h, w2_ref[...], preferred_element_type=F32)

    @pl.when(j == pl.num_programs(1) - 1)
    def _():
        out_ref[...] = jnp.where(used, acc_ref[...], 0.0)


def _moe_experts(block_e, n_used, row_tok, x, w1, w3, w2, *, tm, tf):
    n_rows = row_tok.shape[0]
    d = x.shape[1]
    f = w1.shape[2]
    nf = f // tf
    assert tm % MOE_GATHER_ROWS == 0 and nf >= tm // MOE_GATHER_ROWS

    def ff(i, j, nu):
        return jnp.where(i < nu[0], j, nf - 1)

    return pl.pallas_call(
        functools.partial(_moe_kernel, tm=tm),
        grid_spec=pltpu.PrefetchScalarGridSpec(
            num_scalar_prefetch=3,
            grid=(n_rows // tm, nf),
            in_specs=[
                pl.BlockSpec(memory_space=pl.ANY),
                pl.BlockSpec((None, d, tf), lambda i, j, be, nu, tok: (be[i], 0, ff(i, j, nu))),
                pl.BlockSpec((None, d, tf), lambda i, j, be, nu, tok: (be[i], 0, ff(i, j, nu))),
                pl.BlockSpec((None, tf, d), lambda i, j, be, nu, tok: (be[i], ff(i, j, nu), 0)),
            ],
            out_specs=pl.BlockSpec((tm, d), lambda i, j, be, nu, tok: (i, 0)),
            scratch_shapes=[pltpu.VMEM((2, tm, d), F32), pltpu.VMEM((tm, d), BF16),
                            pltpu.VMEM((tm, d), F32), pltpu.SemaphoreType.DMA((2,))],
        ),
        out_shape=jax.ShapeDtypeStruct((n_rows, d), F32),
        compiler_params=_params("arbitrary", "arbitrary"),
        name="moe_experts",
    )(block_e, n_used, row_tok, x, w1, w3, w2)


COMBINE_UNROLL = 8


def _combine_kernel(dest_ref, dest_next_ref, y_hbm, h_ref, gate_ref, fn_ref, out_ref,
                    buf_ref, sem, *, tc):
    i = pl.program_id(0)
    slot = i % 2

    def fetch(idx_ref, dst_slot):
        def body(r, carry):
            for k in range(TOP_K):
                _row_copy(y_hbm, idx_ref[TOP_K * r + k], buf_ref.at[dst_slot, k], r,
                          sem.at[dst_slot]).start()
            return carry
        lax.fori_loop(0, tc, body, 0, unroll=COMBINE_UNROLL)

    @pl.when(i == 0)
    def _():
        fetch(dest_ref, 0)

    @pl.when(i + 1 < pl.num_programs(0))
    def _():
        fetch(dest_next_ref, 1 - slot)

    for k in range(TOP_K):
        _block_wait(y_hbm, buf_ref.at[slot, k], sem.at[slot])
    gates = gate_ref[...]
    y = h_ref[...] + gates[:, 0:1] * buf_ref[slot, 0] + gates[:, 1:2] * buf_ref[slot, 1]
    out_ref[...] = _rms_norm(y, fn_ref[...])


def _combine(dest, y_rows, h, gates, fn, *, tc):
    m, d = h.shape
    last = m // tc - 1
    return pl.pallas_call(
        functools.partial(_combine_kernel, tc=tc),
        grid=(m // tc,),
        in_specs=[
            pl.BlockSpec((TOP_K * tc,), lambda i: (i,), memory_space=pltpu.SMEM),
            pl.BlockSpec((TOP_K * tc,), lambda i: (jnp.minimum(i + 1, last),),
                         memory_space=pltpu.SMEM),
            pl.BlockSpec(memory_space=pl.ANY),
            pl.BlockSpec((tc, d), lambda i: (i, 0)),
            pl.BlockSpec((tc, LANES), lambda i: (i, 0)),
            pl.BlockSpec((1, d), lambda i: (0, 0)),
        ],
        out_specs=pl.BlockSpec((tc, d), lambda i: (i, 0)),
        out_shape=jax.ShapeDtypeStruct((m, d), F32),
        scratch_shapes=[pltpu.VMEM((2, TOP_K, tc, d), F32), pltpu.SemaphoreType.DMA((2,))],
        compiler_params=_params("arbitrary"),
        name="moe_combine",
    )(dest, dest, y_rows, h, gates, fn)


def _dispatch_plan(top_idx, n_tok, tm):
    flat_e = top_idx.reshape(-1)
    n_asg = flat_e.shape[0]
    onehot = (flat_e[:, None] == jnp.arange(N_EXPERTS, dtype=jnp.int32)[None, :]).astype(jnp.int32)
    csum = jnp.cumsum(onehot, axis=0)
    rank = jnp.sum(onehot * csum, axis=1) - 1
    counts = csum[-1]
    padded = ((counts + tm - 1) // tm) * tm
    pad_end = jnp.cumsum(padded)
    pad_start = pad_end - padded
    dest = (pad_start[flat_e] + rank).astype(jnp.int32)
    n_blocks = n_asg // tm + N_EXPERTS
    n_used = (pad_end[-1] // tm).astype(jnp.int32)
    flat_tok = jnp.arange(n_asg, dtype=jnp.int32) // TOP_K
    row_tok = jnp.zeros((n_blocks * tm,), jnp.int32).at[dest].set(flat_tok)
    blk = jnp.minimum(jnp.arange(n_blocks, dtype=jnp.int32), n_used - 1) * tm
    block_e = jnp.minimum(jnp.searchsorted(pad_end, blk, side="right"), N_EXPERTS - 1).astype(jnp.int32)
    return dest, row_tok, block_e, n_used.reshape(1)


def _pad_lanes(a):
    return jnp.pad(a, [(0, 0)] * (a.ndim - 1) + [(0, LANES - a.shape[-1])])


def kernel(x, ev_norm_mix, ev_w_in, ev_conv_qk, ev_b_gates, ev_ml_norm, ev_w_out, ev_norm_ffn,
           ev_w1, ev_w3, ev_w2, od_norm_mix, od_pw1, od_dw, od_ln_g, od_ln_b, od_pw2,
           od_norm_ffn, od_router, od_router_b, od_moe_w1, od_moe_w3, od_moe_w2, final_norm):
    bsz, s, d = x.shape
    m = bsz * s
    h = x.reshape(m, d)

    w_in = ev_w_in[0]
    gate0 = 4 * ML_WIDTH
    gate1 = gate0 + 2 * ML_HEADS
    w_main = jnp.concatenate([w_in[:, :gate0], w_in[:, gate1:]], axis=1).astype(BF16)
    w_gates = _pad_lanes(w_in[:, gate0:gate1])
    b_gates = _pad_lanes(ev_b_gates[0][None, :])
    proj, gates = _in_proj(h, ev_norm_mix[0][None, :], w_main, w_gates, b_gates, tm=1024, tn=1024)
    proj = proj.reshape(bsz, s, -1)
    n_chunks = s // ML_CHUNK
    gates = gates[:, :2 * ML_HEADS].reshape(bsz, s, 2 * ML_HEADS).transpose(0, 2, 1)
    i_pre = gates[:, :ML_HEADS].reshape(bsz, ML_HEADS, n_chunks, ML_CHUNK)
    f_pre = gates[:, ML_HEADS:].reshape(bsz, ML_HEADS, n_chunks, ML_CHUNK)
    h_ml = _mlstm(proj, ev_conv_qk[0], i_pre, f_pre, ev_ml_norm[0][None, :], hps=2)
    h_sb = _stick_breaking(proj, gate0, tq=256, hps=4)
    w_out = ev_w_out[0].astype(BF16)
    h = _proj2_res(h, h_ml.reshape(m, ML_WIDTH), h_sb.reshape(m, SB_WIDTH),
                   w_out[:ML_WIDTH], w_out[ML_WIDTH:], tm=1024, tn=1024)

    h = _swiglu_res(h, ev_norm_ffn[0][None, :], ev_w1[0].astype(BF16), ev_w3[0].astype(BF16),
                    ev_w2[0].astype(BF16), tm=512, tf=512)

    pw1 = od_pw1[0].astype(BF16)
    cw = pw1.shape[1] // 2
    u = _pw1_glu(h, od_norm_mix[0][None, :], pw1[:, :cw], pw1[:, cw:], tm=1024, tn=512)
    u = _conv_ln(u.reshape(bsz, s, cw), od_dw[0], od_ln_g[0][None, :], od_ln_b[0][None, :], ts=256)
    h = _proj_res(h, u.reshape(m, cw), od_pw2[0].astype(BF16), tm=1024, tn=1024)

    tm_e = 512
    xn, top_idx, top_gate = _router(h, od_norm_ffn[0][None, :], _pad_lanes(od_router[0]),
                                    _pad_lanes(od_router_b[0][None, :]), tm=512)
    dest, row_tok, block_e, n_used = _dispatch_plan(top_idx[:, :TOP_K], m, tm_e)
    y_rows = _moe_experts(block_e, n_used, row_tok, xn, od_moe_w1[0].astype(BF16),
                          od_moe_w3[0].astype(BF16), od_moe_w2[0].astype(BF16), tm=tm_e, tf=512)
    out = _combine(dest, y_rows, h, top_gate, final_norm[None, :], tc=256)
    return out.reshape(bsz, s, d)
```

```python
import functools

import jax
import jax.numpy as jnp
from jax import lax
from jax.experimental import pallas as pl
from jax.experimental.pallas import tpu as pltpu

ML_HEADS = 4
ML_HEAD_DIM = 256
ML_WIDTH = ML_HEADS * ML_HEAD_DIM
SB_HEADS = 8
SB_HEAD_DIM = 128
SB_WIDTH = SB_HEADS * SB_HEAD_DIM
QK_CONV = 4
ML_CHUNK = 128
CONV_WIDTH = 31
N_EXPERTS = 8
TOP_K = 2
EPS = 1e-6

BF16 = jnp.bfloat16
F32 = jnp.float32

LANES = 128
F32_SUBLANES = 8
BF16_SUBLANES = 16
V7X_VMEM_LIMIT = 56 * 1024 * 1024

NT_DIMS = (((1,), (1,)), ((), ()))
TN_DIMS = (((0,), (0,)), ((), ()))


def _params(*semantics):
    return pltpu.CompilerParams(dimension_semantics=semantics, vmem_limit_bytes=V7X_VMEM_LIMIT)


def _rms_norm(x, g):
    return x * lax.rsqrt(jnp.mean(x * x, axis=-1, keepdims=True) + EPS) * g


def _sigmoid(x):
    return 1.0 / (1.0 + jnp.exp(-x))


def _softplus(z):
    return jnp.maximum(z, 0.0) + jnp.log(1.0 + jnp.exp(-jnp.abs(z)))


def _split_bf16(a):
    hi = a.astype(BF16)
    lo = (a - hi.astype(F32)).astype(BF16)
    return hi, lo


def _dot_3pass(x_hi, x_lo, w_hi_ref, w_lo_ref):
    w_hi = w_hi_ref[...]
    return (jnp.dot(x_hi, w_hi, preferred_element_type=F32)
            + jnp.dot(x_lo, w_hi, preferred_element_type=F32)
            + jnp.dot(x_hi, w_lo_ref[...], preferred_element_type=F32))


def _in_proj_kernel(x_ref, g_ref, w_ref, wg_hi_ref, wg_lo_ref, bg_ref, proj_ref, gates_ref, xn_ref):
    @pl.when(pl.program_id(1) == 0)
    def _():
        xn_hi, xn_lo = _split_bf16(_rms_norm(x_ref[...], g_ref[...]))
        xn_ref[...] = xn_hi
        gates_ref[...] = _dot_3pass(xn_hi, xn_lo, wg_hi_ref, wg_lo_ref) + bg_ref[...]

    proj_ref[...] = jnp.dot(xn_ref[...], w_ref[...], preferred_element_type=F32).astype(BF16)


def _in_proj(x, g, w, wg, bg, *, tm, tn):
    m, d = x.shape
    n = w.shape[1]
    wg_hi, wg_lo = _split_bf16(wg)
    return pl.pallas_call(
        _in_proj_kernel,
        grid=(m // tm, n // tn),
        in_specs=[
            pl.BlockSpec((tm, d), lambda i, j: (i, 0)),
            pl.BlockSpec((1, d), lambda i, j: (0, 0)),
            pl.BlockSpec((d, tn), lambda i, j: (0, j)),
            pl.BlockSpec((d, LANES), lambda i, j: (0, 0)),
            pl.BlockSpec((d, LANES), lambda i, j: (0, 0)),
            pl.BlockSpec((1, LANES), lambda i, j: (0, 0)),
        ],
        out_specs=[
            pl.BlockSpec((tm, tn), lambda i, j: (i, j)),
            pl.BlockSpec((tm, LANES), lambda i, j: (i, 0)),
        ],
        out_shape=[jax.ShapeDtypeStruct((m, n), BF16), jax.ShapeDtypeStruct((m, LANES), F32)],
        scratch_shapes=[pltpu.VMEM((tm, d), BF16)],
        compiler_params=_params("parallel", "arbitrary"),
        name="in_proj",
    )(x, g, w, wg_hi, wg_lo, bg)


def _mlstm_kernel(q_ref, k_ref, v_ref, o_ref, cq_ref, ck_ref, i_ref, f_ref, nrm_ref,
                  out_ref, c_state, n_state, m_state, *, hps):
    L = ML_CHUNK
    d = ML_HEAD_DIM
    n_chunks = q_ref.shape[0] // L
    halo = BF16_SUBLANES

    c_state[...] = jnp.zeros_like(c_state)
    n_state[...] = jnp.zeros_like(n_state)
    m_state[...] = jnp.zeros_like(m_state)

    row = lax.broadcasted_iota(jnp.int32, (L, L), 0)
    col = lax.broadcasted_iota(jnp.int32, (L, L), 1)
    causal = col <= row
    eye = col == row

    def conv_swish(x_ref, w_ref, c, lanes):
        start = pl.multiple_of(c * L, L)
        cur = x_ref[pl.ds(start, L), lanes].astype(F32)
        prev_start = pl.multiple_of(jnp.maximum(start - halo, 0), halo)
        prev = x_ref[pl.ds(prev_start, halo), lanes].astype(F32)
        prev = jnp.where(c > 0, prev, 0.0)
        xx = jnp.concatenate([prev, cur], axis=0)
        w = w_ref[:, lanes]
        y = jnp.zeros((L, d), F32)
        for j in range(QK_CONV):
            off = halo - (QK_CONV - 1) + j
            y = y + w[j:j + 1, :] * xx[off:off + L, :]
        return y * _sigmoid(y)

    def to_col(r):
        return jnp.sum(jnp.where(eye, r, 0.0), axis=1, keepdims=True)

    def head_chunk(hh, c):
        lanes = slice(hh * d, (hh + 1) * d)
        start = pl.multiple_of(c * L, L)
        q = conv_swish(q_ref, cq_ref, c, lanes)
        k = conv_swish(k_ref, ck_ref, c, lanes) * (d ** -0.5)
        v = v_ref[pl.ds(start, L), lanes]
        li_row = i_ref[hh, pl.ds(c, 1), :]
        f_row = f_ref[hh, pl.ds(c, 1), :]
        lf_row = -_softplus(-f_row)
        lf_col = to_col(lf_row)
        li_col = to_col(li_row)
        bl_col = jnp.sum(jnp.where(causal, lf_row, 0.0), axis=1, keepdims=True)
        bl_row = jnp.sum(jnp.where(col >= row, lf_col, 0.0), axis=0, keepdims=True)
        g = jnp.sum(lf_row, axis=1, keepdims=True)

        c_prev = c_state[hh]
        n_prev = n_state[hh]
        m_prev = m_state[hh]

        d_log = jnp.where(causal, bl_col - bl_row + li_row, -jnp.inf)
        inter = bl_col + m_prev
        m_t = jnp.maximum(inter, jnp.max(d_log, axis=1, keepdims=True))
        w_intra = jnp.exp(d_log - m_t)
        w_inter = jnp.exp(inter - m_t)
        q_b = q.astype(BF16)
        k_b = k.astype(BF16)
        s_qk = lax.dot_general(q_b, k_b, NT_DIMS, preferred_element_type=F32) * w_intra
        num = (w_inter * jnp.dot(q_b, c_prev.astype(BF16), preferred_element_type=F32)
               + jnp.dot(s_qk.astype(BF16), v, preferred_element_type=F32))
        qn = (w_inter * jnp.sum(q * n_prev, axis=1, keepdims=True)
              + jnp.sum(s_qk, axis=1, keepdims=True))
        denom = jnp.maximum(jnp.abs(qn), jnp.exp(-m_t))
        h = num / denom
        h = h * lax.rsqrt(jnp.mean(h * h, axis=1, keepdims=True) + EPS)
        h = h * nrm_ref[:, lanes] * _sigmoid(o_ref[pl.ds(start, L), lanes].astype(F32))
        out_ref[pl.ds(start, L), lanes] = h.astype(out_ref.dtype)

        a_col = g - bl_col + li_col
        m_loc = jnp.max(a_col, axis=0, keepdims=True)
        kw = k * jnp.exp(a_col - m_loc)
        c_loc = lax.dot_general(kw.astype(BF16), v, TN_DIMS, preferred_element_type=F32)
        n_loc = jnp.sum(kw, axis=0, keepdims=True)
        m_new = jnp.maximum(g + m_prev, m_loc)
        s_prev = jnp.exp(g + m_prev - m_new)
        s_loc = jnp.exp(m_loc - m_new)
        c_state[hh] = s_prev * c_prev + s_loc * c_loc
        n_state[hh] = s_prev * n_prev + s_loc * n_loc
        m_state[hh] = m_new

    def chunk(c, carry):
        for hh in range(hps):
            head_chunk(hh, c)
        return carry

    lax.fori_loop(0, n_chunks, chunk, 0)


def _mlstm(proj, conv_qk, i_pre, f_pre, ml_norm, *, hps):
    b, s, _ = proj.shape
    d = ML_HEAD_DIM
    w = hps * d
    groups = ML_HEADS // hps
    n_chunks = s // ML_CHUNK

    def col_spec(part):
        return pl.BlockSpec((None, s, w), lambda bi, hi: (bi, 0, part * groups + hi))

    gate_spec = pl.BlockSpec((None, hps, n_chunks, ML_CHUNK), lambda bi, hi: (bi, hi, 0, 0))
    return pl.pallas_call(
        functools.partial(_mlstm_kernel, hps=hps),
        grid=(b, groups),
        in_specs=[
            col_spec(0), col_spec(1), col_spec(2), col_spec(3),
            pl.BlockSpec((QK_CONV, w), lambda bi, hi: (0, hi)),
            pl.BlockSpec((QK_CONV, w), lambda bi, hi: (0, groups + hi)),
            gate_spec, gate_spec,
            pl.BlockSpec((1, w), lambda bi, hi: (0, hi)),
        ],
        out_specs=pl.BlockSpec((None, s, w), lambda bi, hi: (bi, 0, hi)),
        out_shape=jax.ShapeDtypeStruct((b, s, ML_WIDTH), BF16),
        scratch_shapes=[pltpu.VMEM((hps, d, d), F32), pltpu.VMEM((hps, 1, d), F32),
                        pltpu.VMEM((hps, 1, 1), F32)],
        compiler_params=_params("parallel", "parallel"),
        name="mlstm",
    )(proj, proj, proj, proj, conv_qk, conv_qk, i_pre, f_pre, ml_norm)


def _sb_kernel(q_ref, k_ref, v_ref, out_ref, *, tq, hps):
    qi = pl.program_id(2)
    d = SB_HEAD_DIM
    scale = d ** -0.5
    row = lax.broadcasted_iota(jnp.int32, (tq, tq), 0)
    col = lax.broadcasted_iota(jnp.int32, (tq, tq), 1)
    strict = col < row
    suffix = (row > col).astype(BF16)

    def block(kb, carry, diag):
        start = pl.multiple_of(kb * tq, tq)
        heads = range(hps)
        lanes = [slice(hh * d, (hh + 1) * d) for hh in heads]
        z = [lax.dot_general(q_ref[:, lanes[hh]], k_ref[pl.ds(start, tq), lanes[hh]], NT_DIMS,
                             preferred_element_type=F32) * scale for hh in heads]
        sp, log_beta, sp_hi, sp_lo = [], [], [], []
        for hh in heads:
            s = _softplus(z[hh])
            log_beta.append(z[hh] - s)
            if diag:
                s = jnp.where(strict, s, 0.0)
            hi = s.astype(BF16)
            sp.append(s)
            sp_hi.append(hi)
            sp_lo.append((s - hi.astype(F32)).astype(BF16))
        rest = [jnp.dot(sp_hi[hh], suffix, preferred_element_type=F32)
                + jnp.dot(sp_lo[hh], suffix, preferred_element_type=F32) for hh in heads]
        w = []
        for hh in heads:
            wh = jnp.exp(log_beta[hh] - rest[hh] - carry[hh][0])
            if diag:
                wh = jnp.where(strict, wh, 0.0)
            w.append(wh.astype(BF16))
        out = []
        for hh in heads:
            acc = carry[hh][1] + jnp.dot(w[hh], v_ref[pl.ds(start, tq), lanes[hh]],
                                         preferred_element_type=F32)
            run = carry[hh][0] + rest[hh][:, 0:1] + sp[hh][:, 0:1]
            out.append((run, acc))
        return tuple(out)

    init = tuple((jnp.zeros((tq, 1), F32), jnp.zeros((tq, d), F32)) for _ in range(hps))
    carry = block(qi, init, True)
    carry = lax.fori_loop(0, qi, lambda t, c: block(qi - 1 - t, c, False), carry)
    for hh in range(hps):
        out_ref[:, hh * d:(hh + 1) * d] = carry[hh][1].astype(out_ref.dtype)


def _stick_breaking(proj, col0, *, tq, hps):
    b, s, _ = proj.shape
    w = hps * SB_HEAD_DIM
    groups = SB_HEADS // hps
    c0 = col0 // w
    return pl.pallas_call(
        functools.partial(_sb_kernel, tq=tq, hps=hps),
        grid=(b, groups, s // tq),
        in_specs=[
            pl.BlockSpec((None, tq, w), lambda bi, hi, qi: (bi, qi, c0 + hi)),
            pl.BlockSpec((None, s, w), lambda bi, hi, qi: (bi, 0, c0 + groups + hi)),
            pl.BlockSpec((None, s, w), lambda bi, hi, qi: (bi, 0, c0 + 2 * groups + hi)),
        ],
        out_specs=pl.BlockSpec((None, tq, w), lambda bi, hi, qi: (bi, qi, hi)),
        out_shape=jax.ShapeDtypeStruct((b, s, SB_WIDTH), BF16),
        compiler_params=_params("parallel", "parallel", "arbitrary"),
        name="stick_breaking",
    )(proj, proj, proj)


def _proj2_res_kernel(x_ref, a_ref, b_ref, wa_ref, wb_ref, out_ref):
    out_ref[...] = (x_ref[...]
                    + jnp.dot(a_ref[...], wa_ref[...], preferred_element_type=F32)
                    + jnp.dot(b_ref[...], wb_ref[...], preferred_element_type=F32))


def _proj2_res(x, a, b, wa, wb, *, tm, tn):
    m, n = x.shape
    ka, kb = a.shape[1], b.shape[1]
    return pl.pallas_call(
        _proj2_res_kernel,
        grid=(m // tm, n // tn),
        in_specs=[
            pl.BlockSpec((tm, tn), lambda i, j: (i, j)),
            pl.BlockSpec((tm, ka), lambda i, j: (i, 0)),
            pl.BlockSpec((tm, kb), lambda i, j: (i, 0)),
            pl.BlockSpec((ka, tn), lambda i, j: (0, j)),
            pl.BlockSpec((kb, tn), lambda i, j: (0, j)),
        ],
        out_specs=pl.BlockSpec((tm, tn), lambda i, j: (i, j)),
        out_shape=jax.ShapeDtypeStruct((m, n), F32),
        compiler_params=_params("parallel", "arbitrary"),
        name="out_proj",
    )(x, a, b, wa, wb)


def _proj_res_kernel(x_ref, a_ref, w_ref, out_ref):
    out_ref[...] = x_ref[...] + jnp.dot(a_ref[...], w_ref[...], preferred_element_type=F32)


def _proj_res(x, a, w, *, tm, tn):
    m, n = x.shape
    k = a.shape[1]
    return pl.pallas_call(
        _proj_res_kernel,
        grid=(m // tm, n // tn),
        in_specs=[
            pl.BlockSpec((tm, tn), lambda i, j: (i, j)),
            pl.BlockSpec((tm, k), lambda i, j: (i, 0)),
            pl.BlockSpec((k, tn), lambda i, j: (0, j)),
        ],
        out_specs=pl.BlockSpec((tm, tn), lambda i, j: (i, j)),
        out_shape=jax.ShapeDtypeStruct((m, n), F32),
        compiler_params=_params("parallel", "arbitrary"),
        name="pw2",
    )(x, a, w)


def _swiglu_kernel(x_ref, g_ref, w1_ref, w3_ref, w2_ref, out_ref, xn_ref, acc_ref):
    j = pl.program_id(1)

    @pl.when(j == 0)
    def _():
        xn_ref[...] = _rms_norm(x_ref[...], g_ref[...]).astype(BF16)
        acc_ref[...] = jnp.zeros_like(acc_ref)

    xn = xn_ref[...]
    a = jnp.dot(xn, w1_ref[...], preferred_element_type=F32)
    b = jnp.dot(xn, w3_ref[...], preferred_element_type=F32)
    h = (a * _sigmoid(a) * b).astype(BF16)
    acc_ref[...] += jnp.dot(h, w2_ref[...], preferred_element_type=F32)

    @pl.when(j == pl.num_programs(1) - 1)
    def _():
        out_ref[...] = x_ref[...] + acc_ref[...]


def _swiglu_res(x, g, w1, w3, w2, *, tm, tf):
    m, d = x.shape
    f = w1.shape[1]
    return pl.pallas_call(
        _swiglu_kernel,
        grid=(m // tm, f // tf),
        in_specs=[
            pl.BlockSpec((tm, d), lambda i, j: (i, 0)),
            pl.BlockSpec((1, d), lambda i, j: (0, 0)),
            pl.BlockSpec((d, tf), lambda i, j: (0, j)),
            pl.BlockSpec((d, tf), lambda i, j: (0, j)),
            pl.BlockSpec((tf, d), lambda i, j: (j, 0)),
        ],
        out_specs=pl.BlockSpec((tm, d), lambda i, j: (i, 0)),
        out_shape=jax.ShapeDtypeStruct((m, d), F32),
        scratch_shapes=[pltpu.VMEM((tm, d), BF16), pltpu.VMEM((tm, d), F32)],
        compiler_params=_params("parallel", "arbitrary"),
        name="swiglu",
    )(x, g, w1, w3, w2)


def _pw1_glu_kernel(x_ref, g_ref, wa_ref, wg_ref, out_ref, xn_ref):
    @pl.when(pl.program_id(1) == 0)
    def _():
        xn_ref[...] = _rms_norm(x_ref[...], g_ref[...]).astype(BF16)

    xn = xn_ref[...]
    a = jnp.dot(xn, wa_ref[...], preferred_element_type=F32)
    gate = jnp.dot(xn, wg_ref[...], preferred_element_type=F32)
    out_ref[...] = (a * _sigmoid(gate)).astype(out_ref.dtype)


def _pw1_glu(x, g, wa, wg, *, tm, tn):
    m, d = x.shape
    n = wa.shape[1]
    return pl.pallas_call(
        _pw1_glu_kernel,
        grid=(m // tm, n // tn),
        in_specs=[
            pl.BlockSpec((tm, d), lambda i, j: (i, 0)),
            pl.BlockSpec((1, d), lambda i, j: (0, 0)),
            pl.BlockSpec((d, tn), lambda i, j: (0, j)),
            pl.BlockSpec((d, tn), lambda i, j: (0, j)),
        ],
        out_specs=pl.BlockSpec((tm, tn), lambda i, j: (i, j)),
        out_shape=jax.ShapeDtypeStruct((m, n), BF16),
        scratch_shapes=[pltpu.VMEM((tm, d), BF16)],
        compiler_params=_params("parallel", "arbitrary"),
        name="pw1_glu",
    )(x, g, wa, wg)


CONV_HALO = 32
CONV_ROWS = 64


def _conv_ln_kernel(halo_ref, u_ref, w_ref, g_ref, b_ref, out_ref, xx_ref, y_ref, *, ts):
    ch = u_ref.shape[1]
    first = pl.program_id(1) == 0
    halo = halo_ref[...].astype(F32)
    xx_ref[pl.ds(0, CONV_HALO), :] = jnp.where(first, 0.0, halo)
    xx_ref[pl.ds(CONV_HALO, ts), :] = u_ref[...].astype(F32)

    span = CONV_ROWS + CONV_HALO
    first_tap = CONV_HALO - (CONV_WIDTH - 1)

    def lane_block(cb, carry):
        c0 = pl.multiple_of(cb * LANES, LANES)
        w = w_ref[:, pl.ds(c0, LANES)]
        for rb in range(ts // CONV_ROWS):
            window = xx_ref[pl.ds(rb * CONV_ROWS, span), pl.ds(c0, LANES)]
            acc = jnp.zeros((CONV_ROWS, LANES), F32)
            for r in range(F32_SUBLANES):
                shifted = window if r == 0 else pltpu.roll(window, span - r, axis=0)
                for off in range(r, CONV_HALO + 1, F32_SUBLANES):
                    j = off - first_tap
                    if 0 <= j < CONV_WIDTH:
                        acc = acc + w[j:j + 1, :] * shifted[off - r:off - r + CONV_ROWS, :]
            y_ref[pl.ds(rb * CONV_ROWS, CONV_ROWS), pl.ds(c0, LANES)] = acc
        return carry

    lax.fori_loop(0, ch // LANES, lane_block, 0)

    y = y_ref[...]
    mu = jnp.mean(y, axis=-1, keepdims=True)
    yc = y - mu
    yn = yc * lax.rsqrt(jnp.mean(yc * yc, axis=-1, keepdims=True) + EPS) * g_ref[...] + b_ref[...]
    out_ref[...] = (yn * _sigmoid(yn)).astype(out_ref.dtype)


def _conv_ln(u, w, g, b, *, ts):
    bsz, s, ch = u.shape
    per = ts // CONV_HALO
    return pl.pallas_call(
        functools.partial(_conv_ln_kernel, ts=ts),
        grid=(bsz, s // ts),
        in_specs=[
            pl.BlockSpec((None, CONV_HALO, ch), lambda bi, i: (bi, jnp.maximum(i * per - 1, 0), 0)),
            pl.BlockSpec((None, ts, ch), lambda bi, i: (bi, i, 0)),
            pl.BlockSpec((CONV_WIDTH, ch), lambda bi, i: (0, 0)),
            pl.BlockSpec((1, ch), lambda bi, i: (0, 0)),
            pl.BlockSpec((1, ch), lambda bi, i: (0, 0)),
        ],
        out_specs=pl.BlockSpec((None, ts, ch), lambda bi, i: (bi, i, 0)),
        out_shape=jax.ShapeDtypeStruct((bsz, s, ch), BF16),
        scratch_shapes=[pltpu.VMEM((CONV_HALO + ts, ch), F32), pltpu.VMEM((ts, ch), F32)],
        compiler_params=_params("parallel", "parallel"),
        name="conv_ln",
    )(u, u, w, g, b)


def _router_kernel(x_ref, g_ref, wr_hi_ref, wr_lo_ref, br_ref, xn_ref, idx_ref, gate_ref):
    xn = _rms_norm(x_ref[...], g_ref[...])
    xn_ref[...] = xn
    xn_hi, xn_lo = _split_bf16(xn)
    logits = _dot_3pass(xn_hi, xn_lo, wr_hi_ref, wr_lo_ref) + br_ref[...]
    lane = lax.broadcasted_iota(jnp.int32, logits.shape, 1)
    logits = jnp.where(lane < N_EXPERTS, logits, -jnp.inf)
    m1 = jnp.max(logits, axis=1, keepdims=True)
    i1 = jnp.min(jnp.where(logits == m1, lane, LANES), axis=1, keepdims=True)
    rest = jnp.where(lane == i1, -jnp.inf, logits)
    m2 = jnp.max(rest, axis=1, keepdims=True)
    i2 = jnp.min(jnp.where(rest == m2, lane, LANES), axis=1, keepdims=True)
    e2 = jnp.exp(m2 - m1)
    g1 = 1.0 / (1.0 + e2)
    g2 = e2 / (1.0 + e2)
    idx_ref[...] = jnp.where(lane == 0, i1, i2)
    gate_ref[...] = jnp.where(lane == 0, g1, g2)


def _router(x, g, wr, br, *, tm):
    m, d = x.shape
    wr_hi, wr_lo = _split_bf16(wr)
    return pl.pallas_call(
        _router_kernel,
        grid=(m // tm,),
        in_specs=[
            pl.BlockSpec((tm, d), lambda i: (i, 0)),
            pl.BlockSpec((1, d), lambda i: (0, 0)),
            pl.BlockSpec((d, LANES), lambda i: (0, 0)),
            pl.BlockSpec((d, LANES), lambda i: (0, 0)),
            pl.BlockSpec((1, LANES), lambda i: (0, 0)),
        ],
        out_specs=[
            pl.BlockSpec((tm, d), lambda i: (i, 0)),
            pl.BlockSpec((tm, LANES), lambda i: (i, 0)),
            pl.BlockSpec((tm, LANES), lambda i: (i, 0)),
        ],
        out_shape=[
            jax.ShapeDtypeStruct((m, d), F32),
            jax.ShapeDtypeStruct((m, LANES), jnp.int32),
            jax.ShapeDtypeStruct((m, LANES), F32),
        ],
        compiler_params=_params("parallel"),
        name="router",
    )(x, g, wr_hi, wr_lo, br)


def _row_copy(src_hbm, src_row, dst_ref, dst_row, sem):
    return pltpu.make_async_copy(src_hbm.at[pl.ds(src_row, 1)], dst_ref.at[pl.ds(dst_row, 1)], sem)


def _block_wait(src_hbm, dst_ref, sem):
    pltpu.make_async_copy(src_hbm.at[pl.ds(0, dst_ref.shape[0])], dst_ref, sem).wait()


MOE_GATHER_ROWS = 64


def _moe_kernel(be_ref, nu_ref, tok_ref, x_hbm, w1_ref, w3_ref, w2_ref, out_ref,
                xg_ref, xb_ref, sem, *, tm):
    i = pl.program_id(0)
    j = pl.program_id(1)
    n_used = nu_ref[0]
    used = i < n_used
    gather_steps = tm // MOE_GATHER_ROWS

    def gather(block, r0):
        for r in range(MOE_GATHER_ROWS):
            row = r0 + r
            _row_copy(x_hbm, tok_ref[block * tm + row], xg_ref, row, sem).start()

    @pl.when(jnp.logical_and(i == 0, j == 0))
    def _():
        def body(c, carry):
            gather(0, c * MOE_GATHER_ROWS)
            return carry
        lax.fori_loop(0, gather_steps, body, 0)

    @pl.when(jnp.logical_and(used, j == 0))
    def _():
        _block_wait(x_hbm, xg_ref, sem)
        xb_ref[...] = xg_ref[...].astype(BF16)

    @pl.when(j == 0)
    def _():
        out_ref[...] = jnp.zeros_like(out_ref)

    @pl.when(jnp.logical_and(i + 1 < n_used, jnp.logical_and(j >= 1, j <= gather_steps)))
    def _():
        gather(i + 1, (j - 1) * MOE_GATHER_ROWS)

    @pl.when(used)
    def _():
        xb = xb_ref[...]
        a = jnp.dot(xb, w1_ref[...].astype(BF16), preferred_element_type=F32)
        b = jnp.dot(xb, w3_ref[...].astype(BF16), preferred_element_type=F32)
        h = (a * _sigmoid(a) * b).astype(BF16)
        out_ref[...] += jnp.dot(h, w2_ref[...].astype(BF16), preferred_element_type=F32)


def _moe_experts(block_e, n_used, row_tok, x, w1, w3, w2, *, tm, tf):
    n_rows = row_tok.shape[0]
    d = x.shape[1]
    f = w1.shape[2]
    nf = f // tf
    assert tm % MOE_GATHER_ROWS == 0 and nf > tm // MOE_GATHER_ROWS

    def ff(i, j, nu):
        return jnp.where(i < nu[0], j, nf - 1)

    return pl.pallas_call(
        functools.partial(_moe_kernel, tm=tm),
        grid_spec=pltpu.PrefetchScalarGridSpec(
            num_scalar_prefetch=3,
            grid=(n_rows // tm, nf),
            in_specs=[
                pl.BlockSpec(memory_space=pl.ANY),
                pl.BlockSpec((None, d, tf), lambda i, j, be, nu, tok: (be[i], 0, ff(i, j, nu))),
                pl.BlockSpec((None, d, tf), lambda i, j, be, nu, tok: (be[i], 0, ff(i, j, nu))),
                pl.BlockSpec((None, tf, d), lambda i, j, be, nu, tok: (be[i], ff(i, j, nu), 0)),
            ],
            out_specs=pl.BlockSpec((tm, d), lambda i, j, be, nu, tok: (i, 0)),
            scratch_shapes=[pltpu.VMEM((tm, d), F32), pltpu.VMEM((tm, d), BF16),
                            pltpu.SemaphoreType.DMA(())],
        ),
        out_shape=jax.ShapeDtypeStruct((n_rows, d), F32),
        compiler_params=_params("arbitrary", "arbitrary"),
        name="moe_experts",
    )(block_e, n_used, row_tok, x, w1, w3, w2)


COMBINE_UNROLL = 8


def _combine_kernel(dest_ref, dest_next_ref, y_hbm, h_ref, gate_ref, fn_ref, out_ref,
                    buf_ref, sem, *, tc):
    i = pl.program_id(0)
    slot = i % 2

    def fetch(idx_ref, dst_slot):
        def body(r, carry):
            for k in range(TOP_K):
                _row_copy(y_hbm, idx_ref[TOP_K * r + k], buf_ref.at[dst_slot, k], r,
                          sem.at[dst_slot]).start()
            return carry
        lax.fori_loop(0, tc, body, 0, unroll=COMBINE_UNROLL)

    @pl.when(i == 0)
    def _():
        fetch(dest_ref, 0)

    @pl.when(i + 1 < pl.num_programs(0))
    def _():
        fetch(dest_next_ref, 1 - slot)

    for k in range(TOP_K):
        _block_wait(y_hbm, buf_ref.at[slot, k], sem.at[slot])
    gates = gate_ref[...]
    y = h_ref[...] + gates[:, 0:1] * buf_ref[slot, 0] + gates[:, 1:2] * buf_ref[slot, 1]
    out_ref[...] = _rms_norm(y, fn_ref[...])


def _combine(dest, y_rows, h, gates, fn, *, tc):
    m, d = h.shape
    last = m // tc - 1
    return pl.pallas_call(
        functools.partial(_combine_kernel, tc=tc),
        grid=(m // tc,),
        in_specs=[
            pl.BlockSpec((TOP_K * tc,), lambda i: (i,), memory_space=pltpu.SMEM),
            pl.BlockSpec((TOP_K * tc,), lambda i: (jnp.minimum(i + 1, last),),
                         memory_space=pltpu.SMEM),
            pl.BlockSpec(memory_space=pl.ANY),
            pl.BlockSpec((tc, d), lambda i: (i, 0)),
            pl.BlockSpec((tc, LANES), lambda i: (i, 0)),
            pl.BlockSpec((1, d), lambda i: (0, 0)),
        ],
        out_specs=pl.BlockSpec((tc, d), lambda i: (i, 0)),
        out_shape=jax.ShapeDtypeStruct((m, d), F32),
        scratch_shapes=[pltpu.VMEM((2, TOP_K, tc, d), F32), pltpu.SemaphoreType.DMA((2,))],
        compiler_params=_params("arbitrary"),
        name="moe_combine",
    )(dest, dest, y_rows, h, gates, fn)


def _dispatch_plan(top_idx, n_tok, tm):
    flat_e = top_idx.reshape(-1)
    n_asg = flat_e.shape[0]
    onehot = (flat_e[:, None] == jnp.arange(N_EXPERTS, dtype=jnp.int32)[None, :]).astype(jnp.int32)
    csum = jnp.cumsum(onehot, axis=0)
    rank = jnp.sum(onehot * csum, axis=1) - 1
    counts = csum[-1]
    padded = ((counts + tm - 1) // tm) * tm
    pad_end = jnp.cumsum(padded)
    pad_start = pad_end - padded
    dest = (pad_start[flat_e] + rank).astype(jnp.int32)
    n_blocks = n_asg // tm + N_EXPERTS
    n_used = (pad_end[-1] // tm).astype(jnp.int32)
    flat_tok = jnp.arange(n_asg, dtype=jnp.int32) // TOP_K
    row_tok = jnp.zeros((n_blocks * tm,), jnp.int32).at[dest].set(flat_tok)
    blk = jnp.minimum(jnp.arange(n_blocks, dtype=jnp.int32), n_used - 1) * tm
    block_e = jnp.minimum(jnp.searchsorted(pad_end, blk, side="right"), N_EXPERTS - 1).astype(jnp.int32)
    return dest, row_tok, block_e, n_used.reshape(1)


def _pad_lanes(a):
    return jnp.pad(a, [(0, 0)] * (a.ndim - 1) + [(0, LANES - a.shape[-1])])


def kernel(x, ev_norm_mix, ev_w_in, ev_conv_qk, ev_b_gates, ev_ml_norm, ev_w_out, ev_norm_ffn,
           ev_w1, ev_w3, ev_w2, od_norm_mix, od_pw1, od_dw, od_ln_g, od_ln_b, od_pw2,
           od_norm_ffn, od_router, od_router_b, od_moe_w1, od_moe_w3, od_moe_w2, final_norm):
    bsz, s, d = x.shape
    m = bsz * s
    h = x.reshape(m, d)

    w_in = ev_w_in[0]
    gate0 = 4 * ML_WIDTH
    gate1 = gate0 + 2 * ML_HEADS
    w_main = jnp.concatenate([w_in[:, :gate0], w_in[:, gate1:]], axis=1).astype(BF16)
    w_gates = _pad_lanes(w_in[:, gate0:gate1])
    b_gates = _pad_lanes(ev_b_gates[0][None, :])
    proj, gates = _in_proj(h, ev_norm_mix[0][None, :], w_main, w_gates, b_gates, tm=1024, tn=1024)
    proj = proj.reshape(bsz, s, -1)
    n_chunks = s // ML_CHUNK
    gates = gates[:, :2 * ML_HEADS].reshape(bsz, s, 2 * ML_HEADS).transpose(0, 2, 1)
    i_pre = gates[:, :ML_HEADS].reshape(bsz, ML_HEADS, n_chunks, ML_CHUNK)
    f_pre = gates[:, ML_HEADS:].reshape(bsz, ML_HEADS, n_chunks, ML_CHUNK)
    h_ml = _mlstm(proj, ev_conv_qk[0], i_pre, f_pre, ev_ml_norm[0][None, :], hps=2)
    h_sb = _stick_breaking(proj, gate0, tq=256, hps=8)
    w_out = ev_w_out[0].astype(BF16)
    h = _proj2_res(h, h_ml.reshape(m, ML_WIDTH), h_sb.reshape(m, SB_WIDTH),
                   w_out[:ML_WIDTH], w_out[ML_WIDTH:], tm=1024, tn=1024)

    h = _swiglu_res(h, ev_norm_ffn[0][None, :], ev_w1[0].astype(BF16), ev_w3[0].astype(BF16),
                    ev_w2[0].astype(BF16), tm=512, tf=512)

    pw1 = od_pw1[0].astype(BF16)
    cw = pw1.shape[1] // 2
    u = _pw1_glu(h, od_norm_mix[0][None, :], pw1[:, :cw], pw1[:, cw:], tm=1024, tn=512)
    u = _conv_ln(u.reshape(bsz, s, cw), od_dw[0], od_ln_g[0][None, :], od_ln_b[0][None, :], ts=256)
    h = _proj_res(h, u.reshape(m, cw), od_pw2[0].astype(BF16), tm=1024, tn=1024)

    tm_e = 1024
    xn, top_idx, top_gate = _router(h, od_norm_ffn[0][None, :], _pad_lanes(od_router[0]),
                                    _pad_lanes(od_router_b[0][None, :]), tm=512)
    dest, row_tok, block_e, n_used = _dispatch_plan(top_idx[:, :TOP_K], m, tm_e)
    y_rows = _moe_experts(block_e, n_used, row_tok, xn, od_moe_w1[0], od_moe_w3[0], od_moe_w2[0],
                          tm=tm_e, tf=256)
    out = _combine(dest, y_rows, h, top_gate, final_norm[None, :], tc=256)
    return out.reshape(bsz, s, d)
```

```python
import functools

import jax
import jax.numpy as jnp
from jax import lax
from jax.experimental import pallas as pl
from jax.experimental.pallas import tpu as pltpu

ML_HEADS = 4
ML_HEAD_DIM = 256
ML_WIDTH = ML_HEADS * ML_HEAD_DIM
SB_HEADS = 8
SB_HEAD_DIM = 128
SB_WIDTH = SB_HEADS * SB_HEAD_DIM
QK_CONV = 4
ML_CHUNK = 128
CONV_WIDTH = 31
N_EXPERTS = 8
TOP_K = 2
EPS = 1e-6

BF16 = jnp.bfloat16
F32 = jnp.float32

LANES = 128
F32_SUBLANES = 8
BF16_SUBLANES = 16
V7X_VMEM_LIMIT = 56 * 1024 * 1024

NT_DIMS = (((1,), (1,)), ((), ()))
TN_DIMS = (((0,), (0,)), ((), ()))


def _params(*semantics):
    return pltpu.CompilerParams(dimension_semantics=semantics, vmem_limit_bytes=V7X_VMEM_LIMIT)


def _rms_norm(x, g):
    return x * lax.rsqrt(jnp.mean(x * x, axis=-1, keepdims=True) + EPS) * g


def _sigmoid(x):
    return 1.0 / (1.0 + jnp.exp(-x))


def _softplus(z):
    return jnp.maximum(z, 0.0) + jnp.log(1.0 + jnp.exp(-jnp.abs(z)))


def _split_bf16(a):
    hi = a.astype(BF16)
    lo = (a - hi.astype(F32)).astype(BF16)
    return hi, lo


def _dot_3pass(x_hi, x_lo, w_hi_ref, w_lo_ref):
    w_hi = w_hi_ref[...]
    return (jnp.dot(x_hi, w_hi, preferred_element_type=F32)
            + jnp.dot(x_lo, w_hi, preferred_element_type=F32)
            + jnp.dot(x_hi, w_lo_ref[...], preferred_element_type=F32))


def _in_proj_kernel(x_ref, g_ref, w_ref, wg_hi_ref, wg_lo_ref, bg_ref, proj_ref, gates_ref, xn_ref):
    @pl.when(pl.program_id(1) == 0)
    def _():
        xn_hi, xn_lo = _split_bf16(_rms_norm(x_ref[...], g_ref[...]))
        xn_ref[...] = xn_hi
        gates_ref[...] = _dot_3pass(xn_hi, xn_lo, wg_hi_ref, wg_lo_ref) + bg_ref[...]

    proj_ref[...] = jnp.dot(xn_ref[...], w_ref[...], preferred_element_type=F32).astype(BF16)


def _in_proj(x, g, w, wg, bg, *, tm, tn):
    m, d = x.shape
    n = w.shape[1]
    wg_hi, wg_lo = _split_bf16(wg)
    return pl.pallas_call(
        _in_proj_kernel,
        grid=(m // tm, n // tn),
        in_specs=[
            pl.BlockSpec((tm, d), lambda i, j: (i, 0)),
            pl.BlockSpec((1, d), lambda i, j: (0, 0)),
            pl.BlockSpec((d, tn), lambda i, j: (0, j)),
            pl.BlockSpec((d, LANES), lambda i, j: (0, 0)),
            pl.BlockSpec((d, LANES), lambda i, j: (0, 0)),
            pl.BlockSpec((1, LANES), lambda i, j: (0, 0)),
        ],
        out_specs=[
            pl.BlockSpec((tm, tn), lambda i, j: (i, j)),
            pl.BlockSpec((tm, LANES), lambda i, j: (i, 0)),
        ],
        out_shape=[jax.ShapeDtypeStruct((m, n), BF16), jax.ShapeDtypeStruct((m, LANES), F32)],
        scratch_shapes=[pltpu.VMEM((tm, d), BF16)],
        compiler_params=_params("parallel", "arbitrary"),
        name="in_proj",
    )(x, g, w, wg_hi, wg_lo, bg)


def _mlstm_kernel(q_ref, k_ref, v_ref, o_ref, cq_ref, ck_ref, i_ref, f_ref, nrm_ref,
                  out_ref, c_state, n_state, m_state, *, hps):
    L = ML_CHUNK
    d = ML_HEAD_DIM
    n_chunks = q_ref.shape[0] // L
    halo = BF16_SUBLANES

    c_state[...] = jnp.zeros_like(c_state)
    n_state[...] = jnp.zeros_like(n_state)
    m_state[...] = jnp.zeros_like(m_state)

    row = lax.broadcasted_iota(jnp.int32, (L, L), 0)
    col = lax.broadcasted_iota(jnp.int32, (L, L), 1)
    causal = col <= row
    eye = col == row

    def conv_swish(x_ref, w_ref, c, lanes):
        start = pl.multiple_of(c * L, L)
        cur = x_ref[pl.ds(start, L), lanes].astype(F32)
        prev_start = pl.multiple_of(jnp.maximum(start - halo, 0), halo)
        prev = x_ref[pl.ds(prev_start, halo), lanes].astype(F32)
        prev = jnp.where(c > 0, prev, 0.0)
        xx = jnp.concatenate([prev, cur], axis=0)
        w = w_ref[:, lanes]
        y = jnp.zeros((L, d), F32)
        for j in range(QK_CONV):
            off = halo - (QK_CONV - 1) + j
            y = y + w[j:j + 1, :] * xx[off:off + L, :]
        return y * _sigmoid(y)

    def to_col(r):
        return jnp.sum(jnp.where(eye, r, 0.0), axis=1, keepdims=True)

    def head_chunk(hh, c):
        lanes = slice(hh * d, (hh + 1) * d)
        start = pl.multiple_of(c * L, L)
        q = conv_swish(q_ref, cq_ref, c, lanes)
        k = conv_swish(k_ref, ck_ref, c, lanes) * (d ** -0.5)
        v = v_ref[pl.ds(start, L), lanes]
        li_row = i_ref[hh, pl.ds(c, 1), :]
        f_row = f_ref[hh, pl.ds(c, 1), :]
        lf_row = -_softplus(-f_row)
        lf_col = to_col(lf_row)
        li_col = to_col(li_row)
        bl_col = jnp.sum(jnp.where(causal, lf_row, 0.0), axis=1, keepdims=True)
        bl_row = jnp.sum(jnp.where(col >= row, lf_col, 0.0), axis=0, keepdims=True)
        g = jnp.sum(lf_row, axis=1, keepdims=True)

        c_prev = c_state[hh]
        n_prev = n_state[hh]
        m_prev = m_state[hh]

        d_log = jnp.where(causal, bl_col - bl_row + li_row, -jnp.inf)
        inter = bl_col + m_prev
        m_t = jnp.maximum(inter, jnp.max(d_log, axis=1, keepdims=True))
        w_intra = jnp.exp(d_log - m_t)
        w_inter = jnp.exp(inter - m_t)
        q_b = q.astype(BF16)
        k_b = k.astype(BF16)
        s_qk = lax.dot_general(q_b, k_b, NT_DIMS, preferred_element_type=F32) * w_intra
        num = (w_inter * jnp.dot(q_b, c_prev.astype(BF16), preferred_element_type=F32)
               + jnp.dot(s_qk.astype(BF16), v, preferred_element_type=F32))
        qn = (w_inter * jnp.sum(q * n_prev, axis=1, keepdims=True)
              + jnp.sum(s_qk, axis=1, keepdims=True))
        denom = jnp.maximum(jnp.abs(qn), jnp.exp(-m_t))
        h = num / denom
        h = h * lax.rsqrt(jnp.mean(h * h, axis=1, keepdims=True) + EPS)
        h = h * nrm_ref[:, lanes] * _sigmoid(o_ref[pl.ds(start, L), lanes].astype(F32))
        out_ref[pl.ds(start, L), lanes] = h.astype(out_ref.dtype)

        a_col = g - bl_col + li_col
        m_loc = jnp.max(a_col, axis=0, keepdims=True)
        kw = k * jnp.exp(a_col - m_loc)
        c_loc = lax.dot_general(kw.astype(BF16), v, TN_DIMS, preferred_element_type=F32)
        n_loc = jnp.sum(kw, axis=0, keepdims=True)
        m_new = jnp.maximum(g + m_prev, m_loc)
        s_prev = jnp.exp(g + m_prev - m_new)
        s_loc = jnp.exp(m_loc - m_new)
        c_state[hh] = s_prev * c_prev + s_loc * c_loc
        n_state[hh] = s_prev * n_prev + s_loc * n_loc
        m_state[hh] = m_new

    def chunk(c, carry):
        for hh in range(hps):
            head_chunk(hh, c)
        return carry

    lax.fori_loop(0, n_chunks, chunk, 0)


def _mlstm(proj, conv_qk, i_pre, f_pre, ml_norm, *, hps):
    b, s, _ = proj.shape
    d = ML_HEAD_DIM
    w = hps * d
    groups = ML_HEADS // hps
    n_chunks = s // ML_CHUNK

    def col_spec(part):
        return pl.BlockSpec((None, s, w), lambda bi, hi: (bi, 0, part * groups + hi))

    gate_spec = pl.BlockSpec((None, hps, n_chunks, ML_CHUNK), lambda bi, hi: (bi, hi, 0, 0))
    return pl.pallas_call(
        functools.partial(_mlstm_kernel, hps=hps),
        grid=(b, groups),
        in_specs=[
            col_spec(0), col_spec(1), col_spec(2), col_spec(3),
            pl.BlockSpec((QK_CONV, w), lambda bi, hi: (0, hi)),
            pl.BlockSpec((QK_CONV, w), lambda bi, hi: (0, groups + hi)),
            gate_spec, gate_spec,
            pl.BlockSpec((1, w), lambda bi, hi: (0, hi)),
        ],
        out_specs=pl.BlockSpec((None, s, w), lambda bi, hi: (bi, 0, hi)),
        out_shape=jax.ShapeDtypeStruct((b, s, ML_WIDTH), BF16),
        scratch_shapes=[pltpu.VMEM((hps, d, d), F32), pltpu.VMEM((hps, 1, d), F32),
                        pltpu.VMEM((hps, 1, 1), F32)],
        compiler_params=_params("parallel", "parallel"),
        name="mlstm",
    )(proj, proj, proj, proj, conv_qk, conv_qk, i_pre, f_pre, ml_norm)


SB_DEAD_LOG = 104.0


def _sb_kernel(q_ref, k_ref, v_ref, out_ref, *, tq, hps):
    qi = pl.program_id(2)
    d = SB_HEAD_DIM
    scale = d ** -0.5
    row = lax.broadcasted_iota(jnp.int32, (tq, tq), 0)
    col = lax.broadcasted_iota(jnp.int32, (tq, tq), 1)
    strict = col < row
    suffix = (row > col).astype(BF16)

    def block(kb, carry, diag):
        start = pl.multiple_of(kb * tq, tq)
        heads = range(hps)
        lanes = [slice(hh * d, (hh + 1) * d) for hh in heads]
        z = [lax.dot_general(q_ref[:, lanes[hh]], k_ref[pl.ds(start, tq), lanes[hh]], NT_DIMS,
                             preferred_element_type=F32) * scale for hh in heads]
        sp, log_beta, sp_hi, sp_lo = [], [], [], []
        for hh in heads:
            s = _softplus(z[hh])
            log_beta.append(z[hh] - s)
            if diag:
                s = jnp.where(strict, s, 0.0)
            hi = s.astype(BF16)
            sp.append(s)
            sp_hi.append(hi)
            sp_lo.append((s - hi.astype(F32)).astype(BF16))
        rest = [jnp.dot(sp_hi[hh], suffix, preferred_element_type=F32)
                + jnp.dot(sp_lo[hh], suffix, preferred_element_type=F32) for hh in heads]
        w = []
        for hh in heads:
            wh = jnp.exp(log_beta[hh] - rest[hh] - carry[hh][0])
            if diag:
                wh = jnp.where(strict, wh, 0.0)
            w.append(wh.astype(BF16))
        out = []
        for hh in heads:
            acc = carry[hh][1] + jnp.dot(w[hh], v_ref[pl.ds(start, tq), lanes[hh]],
                                         preferred_element_type=F32)
            run = carry[hh][0] + rest[hh][:, 0:1] + sp[hh][:, 0:1]
            out.append((run, acc))
        return tuple(out)

    def live(carry):
        low = carry[0][0]
        for hh in range(1, hps):
            low = jnp.minimum(low, carry[hh][0])
        return jnp.min(low) < SB_DEAD_LOG

    init = tuple((jnp.zeros((tq, 1), F32), jnp.zeros((tq, d), F32)) for _ in range(hps))
    carry = block(qi, init, True)

    def step(state):
        t, c, _ = state
        c = block(qi - 1 - t, c, False)
        return t + 1, c, live(c)

    _, carry, _ = lax.while_loop(lambda s: jnp.logical_and(s[0] < qi, s[2]), step,
                                 (jnp.int32(0), carry, live(carry)))
    for hh in range(hps):
        out_ref[:, hh * d:(hh + 1) * d] = carry[hh][1].astype(out_ref.dtype)


def _stick_breaking(proj, col0, *, tq, hps):
    b, s, _ = proj.shape
    w = hps * SB_HEAD_DIM
    groups = SB_HEADS // hps
    c0 = col0 // w
    return pl.pallas_call(
        functools.partial(_sb_kernel, tq=tq, hps=hps),
        grid=(b, groups, s // tq),
        in_specs=[
            pl.BlockSpec((None, tq, w), lambda bi, hi, qi: (bi, qi, c0 + hi)),
            pl.BlockSpec((None, s, w), lambda bi, hi, qi: (bi, 0, c0 + groups + hi)),
            pl.BlockSpec((None, s, w), lambda bi, hi, qi: (bi, 0, c0 + 2 * groups + hi)),
        ],
        out_specs=pl.BlockSpec((None, tq, w), lambda bi, hi, qi: (bi, qi, hi)),
        out_shape=jax.ShapeDtypeStruct((b, s, SB_WIDTH), BF16),
        compiler_params=_params("parallel", "parallel", "arbitrary"),
        name="stick_breaking",
    )(proj, proj, proj)


def _proj2_res_kernel(x_ref, a_ref, b_ref, wa_ref, wb_ref, out_ref):
    out_ref[...] = (x_ref[...]
                    + jnp.dot(a_ref[...], wa_ref[...], preferred_element_type=F32)
                    + jnp.dot(b_ref[...], wb_ref[...], preferred_element_type=F32))


def _proj2_res(x, a, b, wa, wb, *, tm, tn):
    m, n = x.shape
    ka, kb = a.shape[1], b.shape[1]
    return pl.pallas_call(
        _proj2_res_kernel,
        grid=(m // tm, n // tn),
        in_specs=[
            pl.BlockSpec((tm, tn), lambda i, j: (i, j)),
            pl.BlockSpec((tm, ka), lambda i, j: (i, 0)),
            pl.BlockSpec((tm, kb), lambda i, j: (i, 0)),
            pl.BlockSpec((ka, tn), lambda i, j: (0, j)),
            pl.BlockSpec((kb, tn), lambda i, j: (0, j)),
        ],
        out_specs=pl.BlockSpec((tm, tn), lambda i, j: (i, j)),
        out_shape=jax.ShapeDtypeStruct((m, n), F32),
        compiler_params=_params("parallel", "arbitrary"),
        name="out_proj",
    )(x, a, b, wa, wb)


def _proj_res_kernel(x_ref, a_ref, w_ref, out_ref):
    out_ref[...] = x_ref[...] + jnp.dot(a_ref[...], w_ref[...], preferred_element_type=F32)


def _proj_res(x, a, w, *, tm, tn):
    m, n = x.shape
    k = a.shape[1]
    return pl.pallas_call(
        _proj_res_kernel,
        grid=(m // tm, n // tn),
        in_specs=[
            pl.BlockSpec((tm, tn), lambda i, j: (i, j)),
            pl.BlockSpec((tm, k), lambda i, j: (i, 0)),
            pl.BlockSpec((k, tn), lambda i, j: (0, j)),
        ],
        out_specs=pl.BlockSpec((tm, tn), lambda i, j: (i, j)),
        out_shape=jax.ShapeDtypeStruct((m, n), F32),
        compiler_params=_params("parallel", "arbitrary"),
        name="pw2",
    )(x, a, w)


def _swiglu_kernel(x_ref, g_ref, w1_ref, w3_ref, w2_ref, out_ref, xn_ref, acc_ref):
    j = pl.program_id(1)

    @pl.when(j == 0)
    def _():
        xn_ref[...] = _rms_norm(x_ref[...], g_ref[...]).astype(BF16)
        acc_ref[...] = jnp.zeros_like(acc_ref)

    xn = xn_ref[...]
    a = jnp.dot(xn, w1_ref[...], preferred_element_type=F32)
    b = jnp.dot(xn, w3_ref[...], preferred_element_type=F32)
    h = (a * _sigmoid(a) * b).astype(BF16)
    acc_ref[...] += jnp.dot(h, w2_ref[...], preferred_element_type=F32)

    @pl.when(j == pl.num_programs(1) - 1)
    def _():
        out_ref[...] = x_ref[...] + acc_ref[...]


def _swiglu_res(x, g, w1, w3, w2, *, tm, tf):
    m, d = x.shape
    f = w1.shape[1]
    return pl.pallas_call(
        _swiglu_kernel,
        grid=(m // tm, f // tf),
        in_specs=[
            pl.BlockSpec((tm, d), lambda i, j: (i, 0)),
            pl.BlockSpec((1, d), lambda i, j: (0, 0)),
            pl.BlockSpec((d, tf), lambda i, j: (0, j)),
            pl.BlockSpec((d, tf), lambda i, j: (0, j)),
            pl.BlockSpec((tf, d), lambda i, j: (j, 0)),
        ],
        out_specs=pl.BlockSpec((tm, d), lambda i, j: (i, 0)),
        out_shape=jax.ShapeDtypeStruct((m, d), F32),
        scratch_shapes=[pltpu.VMEM((tm, d), BF16), pltpu.VMEM((tm, d), F32)],
        compiler_params=_params("parallel", "arbitrary"),
        name="swiglu",
    )(x, g, w1, w3, w2)


def _pw1_glu_kernel(x_ref, g_ref, wa_ref, wg_ref, out_ref, xn_ref):
    @pl.when(pl.program_id(1) == 0)
    def _():
        xn_ref[...] = _rms_norm(x_ref[...], g_ref[...]).astype(BF16)

    xn = xn_ref[...]
    a = jnp.dot(xn, wa_ref[...], preferred_element_type=F32)
    gate = jnp.dot(xn, wg_ref[...], preferred_element_type=F32)
    out_ref[...] = (a * _sigmoid(gate)).astype(out_ref.dtype)


def _pw1_glu(x, g, wa, wg, *, tm, tn):
    m, d = x.shape
    n = wa.shape[1]
    return pl.pallas_call(
        _pw1_glu_kernel,
        grid=(m // tm, n // tn),
        in_specs=[
            pl.BlockSpec((tm, d), lambda i, j: (i, 0)),
            pl.BlockSpec((1, d), lambda i, j: (0, 0)),
            pl.BlockSpec((d, tn), lambda i, j: (0, j)),
            pl.BlockSpec((d, tn), lambda i, j: (0, j)),
        ],
        out_specs=pl.BlockSpec((tm, tn), lambda i, j: (i, j)),
        out_shape=jax.ShapeDtypeStruct((m, n), BF16),
        scratch_shapes=[pltpu.VMEM((tm, d), BF16)],
        compiler_params=_params("parallel", "arbitrary"),
        name="pw1_glu",
    )(x, g, wa, wg)


CONV_HALO = 32
CONV_ROWS = 64


def _conv_ln_kernel(halo_ref, u_ref, w_ref, g_ref, b_ref, out_ref, xx_ref, y_ref, *, ts):
    ch = u_ref.shape[1]
    first = pl.program_id(1) == 0
    halo = halo_ref[...].astype(F32)
    xx_ref[pl.ds(0, CONV_HALO), :] = jnp.where(first, 0.0, halo)
    xx_ref[pl.ds(CONV_HALO, ts), :] = u_ref[...].astype(F32)

    span = CONV_ROWS + CONV_HALO
    first_tap = CONV_HALO - (CONV_WIDTH - 1)

    def lane_block(cb, carry):
        c0 = pl.multiple_of(cb * LANES, LANES)
        w = w_ref[:, pl.ds(c0, LANES)]
        for rb in range(ts // CONV_ROWS):
            window = xx_ref[pl.ds(rb * CONV_ROWS, span), pl.ds(c0, LANES)]
            acc = jnp.zeros((CONV_ROWS, LANES), F32)
            for r in range(F32_SUBLANES):
                shifted = window if r == 0 else pltpu.roll(window, span - r, axis=0)
                for off in range(r, CONV_HALO + 1, F32_SUBLANES):
                    j = off - first_tap
                    if 0 <= j < CONV_WIDTH:
                        acc = acc + w[j:j + 1, :] * shifted[off - r:off - r + CONV_ROWS, :]
            y_ref[pl.ds(rb * CONV_ROWS, CONV_ROWS), pl.ds(c0, LANES)] = acc
        return carry

    lax.fori_loop(0, ch // LANES, lane_block, 0)

    y = y_ref[...]
    mu = jnp.mean(y, axis=-1, keepdims=True)
    yc = y - mu
    yn = yc * lax.rsqrt(jnp.mean(yc * yc, axis=-1, keepdims=True) + EPS) * g_ref[...] + b_ref[...]
    out_ref[...] = (yn * _sigmoid(yn)).astype(out_ref.dtype)


def _conv_ln(u, w, g, b, *, ts):
    bsz, s, ch = u.shape
    per = ts // CONV_HALO
    return pl.pallas_call(
        functools.partial(_conv_ln_kernel, ts=ts),
        grid=(bsz, s // ts),
        in_specs=[
            pl.BlockSpec((None, CONV_HALO, ch), lambda bi, i: (bi, jnp.maximum(i * per - 1, 0), 0)),
            pl.BlockSpec((None, ts, ch), lambda bi, i: (bi, i, 0)),
            pl.BlockSpec((CONV_WIDTH, ch), lambda bi, i: (0, 0)),
            pl.BlockSpec((1, ch), lambda bi, i: (0, 0)),
            pl.BlockSpec((1, ch), lambda bi, i: (0, 0)),
        ],
        out_specs=pl.BlockSpec((None, ts, ch), lambda bi, i: (bi, i, 0)),
        out_shape=jax.ShapeDtypeStruct((bsz, s, ch), BF16),
        scratch_shapes=[pltpu.VMEM((CONV_HALO + ts, ch), F32), pltpu.VMEM((ts, ch), F32)],
        compiler_params=_params("parallel", "parallel"),
        name="conv_ln",
    )(u, u, w, g, b)


def _router_kernel(x_ref, g_ref, wr_hi_ref, wr_lo_ref, br_ref, xn_ref, idx_ref, gate_ref):
    xn = _rms_norm(x_ref[...], g_ref[...])
    xn_ref[...] = xn
    xn_hi, xn_lo = _split_bf16(xn)
    logits = _dot_3pass(xn_hi, xn_lo, wr_hi_ref, wr_lo_ref) + br_ref[...]
    lane = lax.broadcasted_iota(jnp.int32, logits.shape, 1)
    logits = jnp.where(lane < N_EXPERTS, logits, -jnp.inf)
    m1 = jnp.max(logits, axis=1, keepdims=True)
    i1 = jnp.min(jnp.where(logits == m1, lane, LANES), axis=1, keepdims=True)
    rest = jnp.where(lane == i1, -jnp.inf, logits)
    m2 = jnp.max(rest, axis=1, keepdims=True)
    i2 = jnp.min(jnp.where(rest == m2, lane, LANES), axis=1, keepdims=True)
    e2 = jnp.exp(m2 - m1)
    g1 = 1.0 / (1.0 + e2)
    g2 = e2 / (1.0 + e2)
    idx_ref[...] = jnp.where(lane == 0, i1, i2)
    gate_ref[...] = jnp.where(lane == 0, g1, g2)


def _router(x, g, wr, br, *, tm):
    m, d = x.shape
    wr_hi, wr_lo = _split_bf16(wr)
    return pl.pallas_call(
        _router_kernel,
        grid=(m // tm,),
        in_specs=[
            pl.BlockSpec((tm, d), lambda i: (i, 0)),
            pl.BlockSpec((1, d), lambda i: (0, 0)),
            pl.BlockSpec((d, LANES), lambda i: (0, 0)),
            pl.BlockSpec((d, LANES), lambda i: (0, 0)),
            pl.BlockSpec((1, LANES), lambda i: (0, 0)),
        ],
        out_specs=[
            pl.BlockSpec((tm, d), lambda i: (i, 0)),
            pl.BlockSpec((tm, LANES), lambda i: (i, 0)),
            pl.BlockSpec((tm, LANES), lambda i: (i, 0)),
        ],
        out_shape=[
            jax.ShapeDtypeStruct((m, d), F32),
            jax.ShapeDtypeStruct((m, LANES), jnp.int32),
            jax.ShapeDtypeStruct((m, LANES), F32),
        ],
        compiler_params=_params("parallel"),
        name="router",
    )(x, g, wr_hi, wr_lo, br)


def _row_copy(src_hbm, src_row, dst_ref, dst_row, sem):
    return pltpu.make_async_copy(src_hbm.at[pl.ds(src_row, 1)], dst_ref.at[pl.ds(dst_row, 1)], sem)


def _block_wait(src_hbm, dst_ref, sem):
    pltpu.make_async_copy(src_hbm.at[pl.ds(0, dst_ref.shape[0])], dst_ref, sem).wait()


MOE_GATHER_ROWS = 64


def _moe_kernel(be_ref, nu_ref, rows_ref, tok_ref, x_hbm, w1_ref, w3_ref, w2_ref, out_ref,
                xg_ref, xb_ref, sem, *, tm):
    i = pl.program_id(0)
    j = pl.program_id(1)
    n_used = nu_ref[0]
    used = i < n_used
    half = rows_ref[i] <= tm // 2
    gather_steps = tm // MOE_GATHER_ROWS
    gathering = jnp.logical_and(i + 1 < n_used, jnp.logical_and(j >= 1, j <= gather_steps))

    def gather(block, r0):
        for r in range(MOE_GATHER_ROWS):
            row = r0 + r
            _row_copy(x_hbm, tok_ref[block * tm + row], xg_ref, row, sem).start()

    @pl.when(jnp.logical_and(i == 0, j == 0))
    def _():
        def body(c, carry):
            gather(0, c * MOE_GATHER_ROWS)
            return carry
        lax.fori_loop(0, gather_steps, body, 0)

    @pl.when(jnp.logical_and(used, j == 0))
    def _():
        _block_wait(x_hbm, xg_ref, sem)
        xb_ref[...] = xg_ref[...].astype(BF16)

    @pl.when(j == 0)
    def _():
        out_ref[...] = jnp.zeros_like(out_ref)

    def compute(m, with_gather):
        if with_gather:
            gather(i + 1, (j - 1) * MOE_GATHER_ROWS)
        xb = xb_ref[pl.ds(0, m), :]
        a = jnp.dot(xb, w1_ref[...].astype(BF16), preferred_element_type=F32)
        b = jnp.dot(xb, w3_ref[...].astype(BF16), preferred_element_type=F32)
        h = (a * _sigmoid(a) * b).astype(BF16)
        out_ref[pl.ds(0, m), :] += jnp.dot(h, w2_ref[...].astype(BF16), preferred_element_type=F32)

    for m, is_half in ((tm // 2, True), (tm, False)):
        for with_gather in (True, False):
            cond = jnp.logical_and(jnp.logical_and(used, half == is_half), gathering == with_gather)
            pl.when(cond)(functools.partial(compute, m, with_gather))


def _moe_experts(block_e, n_used, block_rows, row_tok, x, w1, w3, w2, *, tm, tf):
    n_rows = row_tok.shape[0]
    d = x.shape[1]
    f = w1.shape[2]
    nf = f // tf
    assert tm % MOE_GATHER_ROWS == 0 and nf > tm // MOE_GATHER_ROWS

    def ff(i, j, nu):
        return jnp.where(i < nu[0], j, nf - 1)

    return pl.pallas_call(
        functools.partial(_moe_kernel, tm=tm),
        grid_spec=pltpu.PrefetchScalarGridSpec(
            num_scalar_prefetch=4,
            grid=(n_rows // tm, nf),
            in_specs=[
                pl.BlockSpec(memory_space=pl.ANY),
                pl.BlockSpec((None, d, tf), lambda i, j, be, nu, rows, tok: (be[i], 0, ff(i, j, nu))),
                pl.BlockSpec((None, d, tf), lambda i, j, be, nu, rows, tok: (be[i], 0, ff(i, j, nu))),
                pl.BlockSpec((None, tf, d), lambda i, j, be, nu, rows, tok: (be[i], ff(i, j, nu), 0)),
            ],
            out_specs=pl.BlockSpec((tm, d), lambda i, j, be, nu, rows, tok: (i, 0)),
            scratch_shapes=[pltpu.VMEM((tm, d), F32), pltpu.VMEM((tm, d), BF16),
                            pltpu.SemaphoreType.DMA(())],
        ),
        out_shape=jax.ShapeDtypeStruct((n_rows, d), F32),
        compiler_params=_params("arbitrary", "arbitrary"),
        name="moe_experts",
    )(block_e, n_used, block_rows, row_tok, x, w1, w3, w2)


COMBINE_UNROLL = 8


def _combine_kernel(dest_ref, dest_next_ref, y_hbm, h_ref, gate_ref, fn_ref, out_ref,
                    buf_ref, sem, *, tc):
    i = pl.program_id(0)
    slot = i % 2

    def fetch(idx_ref, dst_slot):
        def body(r, carry):
            for k in range(TOP_K):
                _row_copy(y_hbm, idx_ref[TOP_K * r + k], buf_ref.at[dst_slot, k], r,
                          sem.at[dst_slot]).start()
            return carry
        lax.fori_loop(0, tc, body, 0, unroll=COMBINE_UNROLL)

    @pl.when(i == 0)
    def _():
        fetch(dest_ref, 0)

    @pl.when(i + 1 < pl.num_programs(0))
    def _():
        fetch(dest_next_ref, 1 - slot)

    for k in range(TOP_K):
        _block_wait(y_hbm, buf_ref.at[slot, k], sem.at[slot])
    gates = gate_ref[...]
    y = h_ref[...] + gates[:, 0:1] * buf_ref[slot, 0] + gates[:, 1:2] * buf_ref[slot, 1]
    out_ref[...] = _rms_norm(y, fn_ref[...])


def _combine(dest, y_rows, h, gates, fn, *, tc):
    m, d = h.shape
    last = m // tc - 1
    return pl.pallas_call(
        functools.partial(_combine_kernel, tc=tc),
        grid=(m // tc,),
        in_specs=[
            pl.BlockSpec((TOP_K * tc,), lambda i: (i,), memory_space=pltpu.SMEM),
            pl.BlockSpec((TOP_K * tc,), lambda i: (jnp.minimum(i + 1, last),),
                         memory_space=pltpu.SMEM),
            pl.BlockSpec(memory_space=pl.ANY),
            pl.BlockSpec((tc, d), lambda i: (i, 0)),
            pl.BlockSpec((tc, LANES), lambda i: (i, 0)),
            pl.BlockSpec((1, d), lambda i: (0, 0)),
        ],
        out_specs=pl.BlockSpec((tc, d), lambda i: (i, 0)),
        out_shape=jax.ShapeDtypeStruct((m, d), F32),
        scratch_shapes=[pltpu.VMEM((2, TOP_K, tc, d), F32), pltpu.SemaphoreType.DMA((2,))],
        compiler_params=_params("arbitrary"),
        name="moe_combine",
    )(dest, dest, y_rows, h, gates, fn)


def _dispatch_plan(top_idx, n_tok, tm):
    flat_e = top_idx.reshape(-1)
    n_asg = flat_e.shape[0]
    onehot = (flat_e[:, None] == jnp.arange(N_EXPERTS, dtype=jnp.int32)[None, :]).astype(jnp.int32)
    csum = jnp.cumsum(onehot, axis=0)
    rank = jnp.sum(onehot * csum, axis=1) - 1
    counts = csum[-1]
    padded = ((counts + tm - 1) // tm) * tm
    pad_end = jnp.cumsum(padded)
    pad_start = pad_end - padded
    dest = (pad_start[flat_e] + rank).astype(jnp.int32)
    n_blocks = n_asg // tm + N_EXPERTS
    n_used = (pad_end[-1] // tm).astype(jnp.int32)
    flat_tok = jnp.arange(n_asg, dtype=jnp.int32) // TOP_K
    row_tok = jnp.zeros((n_blocks * tm,), jnp.int32).at[dest].set(flat_tok)
    blk = jnp.minimum(jnp.arange(n_blocks, dtype=jnp.int32), n_used - 1) * tm
    block_e = jnp.minimum(jnp.searchsorted(pad_end, blk, side="right"), N_EXPERTS - 1).astype(jnp.int32)
    first = jnp.arange(n_blocks, dtype=jnp.int32) * tm
    block_rows = jnp.clip(pad_start[block_e] + counts[block_e] - first, 0, tm).astype(jnp.int32)
    return dest, row_tok, block_e, block_rows, n_used.reshape(1)


def _pad_lanes(a):
    return jnp.pad(a, [(0, 0)] * (a.ndim - 1) + [(0, LANES - a.shape[-1])])


def kernel(x, ev_norm_mix, ev_w_in, ev_conv_qk, ev_b_gates, ev_ml_norm, ev_w_out, ev_norm_ffn,
           ev_w1, ev_w3, ev_w2, od_norm_mix, od_pw1, od_dw, od_ln_g, od_ln_b, od_pw2,
           od_norm_ffn, od_router, od_router_b, od_moe_w1, od_moe_w3, od_moe_w2, final_norm):
    bsz, s, d = x.shape
    m = bsz * s
    h = x.reshape(m, d)

    w_in = ev_w_in[0]
    gate0 = 4 * ML_WIDTH
    gate1 = gate0 + 2 * ML_HEADS
    w_main = jnp.concatenate([w_in[:, :gate0], w_in[:, gate1:]], axis=1).astype(BF16)
    w_gates = _pad_lanes(w_in[:, gate0:gate1])
    b_gates = _pad_lanes(ev_b_gates[0][None, :])
    proj, gates = _in_proj(h, ev_norm_mix[0][None, :], w_main, w_gates, b_gates, tm=1024, tn=1024)
    proj = proj.reshape(bsz, s, -1)
    n_chunks = s // ML_CHUNK
    gates = gates[:, :2 * ML_HEADS].reshape(bsz, s, 2 * ML_HEADS).transpose(0, 2, 1)
    i_pre = gates[:, :ML_HEADS].reshape(bsz, ML_HEADS, n_chunks, ML_CHUNK)
    f_pre = gates[:, ML_HEADS:].reshape(bsz, ML_HEADS, n_chunks, ML_CHUNK)
    h_ml = _mlstm(proj, ev_conv_qk[0], i_pre, f_pre, ev_ml_norm[0][None, :], hps=2)
    h_sb = _stick_breaking(proj, gate0, tq=256, hps=8)
    w_out = ev_w_out[0].astype(BF16)
    h = _proj2_res(h, h_ml.reshape(m, ML_WIDTH), h_sb.reshape(m, SB_WIDTH),
                   w_out[:ML_WIDTH], w_out[ML_WIDTH:], tm=1024, tn=1024)

    h = _swiglu_res(h, ev_norm_ffn[0][None, :], ev_w1[0].astype(BF16), ev_w3[0].astype(BF16),
                    ev_w2[0].astype(BF16), tm=512, tf=512)

    pw1 = od_pw1[0].astype(BF16)
    cw = pw1.shape[1] // 2
    u = _pw1_glu(h, od_norm_mix[0][None, :], pw1[:, :cw], pw1[:, cw:], tm=1024, tn=512)
    u = _conv_ln(u.reshape(bsz, s, cw), od_dw[0], od_ln_g[0][None, :], od_ln_b[0][None, :], ts=256)
    h = _proj_res(h, u.reshape(m, cw), od_pw2[0].astype(BF16), tm=1024, tn=1024)

    tm_e = 1024
    xn, top_idx, top_gate = _router(h, od_norm_ffn[0][None, :], _pad_lanes(od_router[0]),
                                    _pad_lanes(od_router_b[0][None, :]), tm=512)
    dest, row_tok, block_e, block_rows, n_used = _dispatch_plan(top_idx[:, :TOP_K], m, tm_e)
    y_rows = _moe_experts(block_e, n_used, block_rows, row_tok, xn,
                          od_moe_w1[0], od_moe_w3[0], od_moe_w2[0],
                          tm=tm_e, tf=256)
    out = _combine(dest, y_rows, h, top_gate, final_norm[None, :], tc=256)
    return out.reshape(bsz, s, d)
```

```python
import functools

import jax
import jax.numpy as jnp
from jax import lax
from jax.experimental import pallas as pl
from jax.experimental.pallas import tpu as pltpu

ML_HEADS = 4
ML_HEAD_DIM = 256
ML_WIDTH = ML_HEADS * ML_HEAD_DIM
SB_HEADS = 8
SB_HEAD_DIM = 128
SB_WIDTH = SB_HEADS * SB_HEAD_DIM
QK_CONV = 4
ML_CHUNK = 128
CONV_WIDTH = 31
N_EXPERTS = 8
TOP_K = 2
EPS = 1e-6

BF16 = jnp.bfloat16
F32 = jnp.float32

LANES = 128
F32_SUBLANES = 8
BF16_SUBLANES = 16
V7X_VMEM_LIMIT = 56 * 1024 * 1024

NT_DIMS = (((1,), (1,)), ((), ()))
TN_DIMS = (((0,), (0,)), ((), ()))


def _params(*semantics):
    return pltpu.CompilerParams(dimension_semantics=semantics, vmem_limit_bytes=V7X_VMEM_LIMIT)


def _rms_norm(x, g):
    return x * lax.rsqrt(jnp.mean(x * x, axis=-1, keepdims=True) + EPS) * g


def _sigmoid(x):
    return 1.0 / (1.0 + jnp.exp(-x))


def _softplus(z):
    return jnp.maximum(z, 0.0) + jnp.log(1.0 + jnp.exp(-jnp.abs(z)))


def _split_bf16(a):
    hi = a.astype(BF16)
    lo = (a - hi.astype(F32)).astype(BF16)
    return hi, lo


def _dot_3pass(x_hi, x_lo, w_hi_ref, w_lo_ref):
    w_hi = w_hi_ref[...]
    return (jnp.dot(x_hi, w_hi, preferred_element_type=F32)
            + jnp.dot(x_lo, w_hi, preferred_element_type=F32)
            + jnp.dot(x_hi, w_lo_ref[...], preferred_element_type=F32))


def _in_proj_kernel(x_ref, g_ref, w_ref, wg_hi_ref, wg_lo_ref, bg_ref, proj_ref, gates_ref, xn_ref):
    @pl.when(pl.program_id(1) == 0)
    def _():
        xn_hi, xn_lo = _split_bf16(_rms_norm(x_ref[...], g_ref[...]))
        xn_ref[...] = xn_hi
        gates_ref[...] = _dot_3pass(xn_hi, xn_lo, wg_hi_ref, wg_lo_ref) + bg_ref[...]

    proj_ref[...] = jnp.dot(xn_ref[...], w_ref[...], preferred_element_type=F32).astype(BF16)


def _in_proj(x, g, w, wg, bg, *, tm, tn):
    m, d = x.shape
    n = w.shape[1]
    wg_hi, wg_lo = _split_bf16(wg)
    return pl.pallas_call(
        _in_proj_kernel,
        grid=(m // tm, n // tn),
        in_specs=[
            pl.BlockSpec((tm, d), lambda i, j: (i, 0)),
            pl.BlockSpec((1, d), lambda i, j: (0, 0)),
            pl.BlockSpec((d, tn), lambda i, j: (0, j)),
            pl.BlockSpec((d, LANES), lambda i, j: (0, 0)),
            pl.BlockSpec((d, LANES), lambda i, j: (0, 0)),
            pl.BlockSpec((1, LANES), lambda i, j: (0, 0)),
        ],
        out_specs=[
            pl.BlockSpec((tm, tn), lambda i, j: (i, j)),
            pl.BlockSpec((tm, LANES), lambda i, j: (i, 0)),
        ],
        out_shape=[jax.ShapeDtypeStruct((m, n), BF16), jax.ShapeDtypeStruct((m, LANES), F32)],
        scratch_shapes=[pltpu.VMEM((tm, d), BF16)],
        compiler_params=_params("parallel", "arbitrary"),
        name="in_proj",
    )(x, g, w, wg_hi, wg_lo, bg)


def _mlstm_kernel(q_ref, k_ref, v_ref, o_ref, cq_ref, ck_ref, i_ref, f_ref, nrm_ref,
                  out_ref, c_state, n_state, m_state, *, hps):
    L = ML_CHUNK
    d = ML_HEAD_DIM
    n_chunks = q_ref.shape[0] // L
    halo = BF16_SUBLANES

    c_state[...] = jnp.zeros_like(c_state)
    n_state[...] = jnp.zeros_like(n_state)
    m_state[...] = jnp.zeros_like(m_state)

    row = lax.broadcasted_iota(jnp.int32, (L, L), 0)
    col = lax.broadcasted_iota(jnp.int32, (L, L), 1)
    causal = col <= row
    eye = col == row

    def conv_swish(x_ref, w_ref, c, lanes):
        start = pl.multiple_of(c * L, L)
        cur = x_ref[pl.ds(start, L), lanes].astype(F32)
        prev_start = pl.multiple_of(jnp.maximum(start - halo, 0), halo)
        prev = x_ref[pl.ds(prev_start, halo), lanes].astype(F32)
        prev = jnp.where(c > 0, prev, 0.0)
        xx = jnp.concatenate([prev, cur], axis=0)
        w = w_ref[:, lanes]
        y = jnp.zeros((L, d), F32)
        for j in range(QK_CONV):
            off = halo - (QK_CONV - 1) + j
            r = off % F32_SUBLANES
            shifted = xx if r == 0 else pltpu.roll(xx, halo + L - r, axis=0)
            y = y + w[j:j + 1, :] * shifted[off - r:off - r + L, :]
        return y * _sigmoid(y)

    def to_col(r):
        return jnp.sum(jnp.where(eye, r, 0.0), axis=1, keepdims=True)

    def head_chunk(hh, c):
        lanes = slice(hh * d, (hh + 1) * d)
        start = pl.multiple_of(c * L, L)
        q = conv_swish(q_ref, cq_ref, c, lanes)
        k = conv_swish(k_ref, ck_ref, c, lanes) * (d ** -0.5)
        v = v_ref[pl.ds(start, L), lanes]
        li_row = i_ref[hh, pl.ds(c, 1), :]
        f_row = f_ref[hh, pl.ds(c, 1), :]
        lf_row = -_softplus(-f_row)
        lf_col = to_col(lf_row)
        li_col = to_col(li_row)
        bl_col = jnp.sum(jnp.where(causal, lf_row, 0.0), axis=1, keepdims=True)
        bl_row = jnp.sum(jnp.where(col >= row, lf_col, 0.0), axis=0, keepdims=True)
        g = jnp.sum(lf_row, axis=1, keepdims=True)

        c_prev = c_state[hh]
        n_prev = n_state[hh]
        m_prev = m_state[hh]

        d_log = jnp.where(causal, bl_col - bl_row + li_row, -jnp.inf)
        inter = bl_col + m_prev
        m_t = jnp.maximum(inter, jnp.max(d_log, axis=1, keepdims=True))
        w_intra = jnp.exp(d_log - m_t)
        w_inter = jnp.exp(inter - m_t)
        q_b = q.astype(BF16)
        k_b = k.astype(BF16)
        s_qk = lax.dot_general(q_b, k_b, NT_DIMS, preferred_element_type=F32) * w_intra
        num = (w_inter * jnp.dot(q_b, c_prev.astype(BF16), preferred_element_type=F32)
               + jnp.dot(s_qk.astype(BF16), v, preferred_element_type=F32))
        qn = (w_inter * jnp.sum(q * n_prev, axis=1, keepdims=True)
              + jnp.sum(s_qk, axis=1, keepdims=True))
        denom = jnp.maximum(jnp.abs(qn), jnp.exp(-m_t))
        h = num / denom
        h = h * lax.rsqrt(jnp.mean(h * h, axis=1, keepdims=True) + EPS)
        h = h * nrm_ref[:, lanes] * _sigmoid(o_ref[pl.ds(start, L), lanes].astype(F32))
        out_ref[pl.ds(start, L), lanes] = h.astype(out_ref.dtype)

        a_col = g - bl_col + li_col
        m_loc = jnp.max(a_col, axis=0, keepdims=True)
        kw = k * jnp.exp(a_col - m_loc)
        c_loc = lax.dot_general(kw.astype(BF16), v, TN_DIMS, preferred_element_type=F32)
        n_loc = jnp.sum(kw, axis=0, keepdims=True)
        m_new = jnp.maximum(g + m_prev, m_loc)
        s_prev = jnp.exp(g + m_prev - m_new)
        s_loc = jnp.exp(m_loc - m_new)
        c_state[hh] = s_prev * c_prev + s_loc * c_loc
        n_state[hh] = s_prev * n_prev + s_loc * n_loc
        m_state[hh] = m_new

    def chunk(c, carry):
        for hh in range(hps):
            head_chunk(hh, c)
        return carry

    lax.fori_loop(0, n_chunks, chunk, 0)


def _mlstm(proj, conv_qk, i_pre, f_pre, ml_norm, *, hps):
    b, s, _ = proj.shape
    d = ML_HEAD_DIM
    w = hps * d
    groups = ML_HEADS // hps
    n_chunks = s // ML_CHUNK

    def col_spec(part):
        return pl.BlockSpec((None, s, w), lambda bi, hi: (bi, 0, part * groups + hi))

    gate_spec = pl.BlockSpec((None, hps, n_chunks, ML_CHUNK), lambda bi, hi: (bi, hi, 0, 0))
    return pl.pallas_call(
        functools.partial(_mlstm_kernel, hps=hps),
        grid=(b, groups),
        in_specs=[
            col_spec(0), col_spec(1), col_spec(2), col_spec(3),
            pl.BlockSpec((QK_CONV, w), lambda bi, hi: (0, hi)),
            pl.BlockSpec((QK_CONV, w), lambda bi, hi: (0, groups + hi)),
            gate_spec, gate_spec,
            pl.BlockSpec((1, w), lambda bi, hi: (0, hi)),
        ],
        out_specs=pl.BlockSpec((None, s, w), lambda bi, hi: (bi, 0, hi)),
        out_shape=jax.ShapeDtypeStruct((b, s, ML_WIDTH), BF16),
        scratch_shapes=[pltpu.VMEM((hps, d, d), F32), pltpu.VMEM((hps, 1, d), F32),
                        pltpu.VMEM((hps, 1, 1), F32)],
        compiler_params=_params("parallel", "parallel"),
        name="mlstm",
    )(proj, proj, proj, proj, conv_qk, conv_qk, i_pre, f_pre, ml_norm)


SB_DEAD_LOG = 104.0


def _sb_kernel(q_ref, k_ref, v_ref, out_ref, *, tq, hps):
    qi = pl.program_id(2)
    d = SB_HEAD_DIM
    scale = d ** -0.5
    row = lax.broadcasted_iota(jnp.int32, (tq, tq), 0)
    col = lax.broadcasted_iota(jnp.int32, (tq, tq), 1)
    strict = col < row
    suffix = (row > col).astype(BF16)

    def block(kb, carry, diag):
        start = pl.multiple_of(kb * tq, tq)
        heads = range(hps)
        lanes = [slice(hh * d, (hh + 1) * d) for hh in heads]
        z = [lax.dot_general(q_ref[:, lanes[hh]], k_ref[pl.ds(start, tq), lanes[hh]], NT_DIMS,
                             preferred_element_type=F32) * scale for hh in heads]
        sp, log_beta, sp_hi, sp_lo = [], [], [], []
        for hh in heads:
            s = _softplus(z[hh])
            log_beta.append(z[hh] - s)
            if diag:
                s = jnp.where(strict, s, 0.0)
            hi = s.astype(BF16)
            sp.append(s)
            sp_hi.append(hi)
            sp_lo.append((s - hi.astype(F32)).astype(BF16))
        rest = [jnp.dot(sp_hi[hh], suffix, preferred_element_type=F32)
                + jnp.dot(sp_lo[hh], suffix, preferred_element_type=F32) for hh in heads]
        w = []
        for hh in heads:
            wh = jnp.exp(log_beta[hh] - rest[hh] - carry[hh][0])
            if diag:
                wh = jnp.where(strict, wh, 0.0)
            w.append(wh.astype(BF16))
        out = []
        for hh in heads:
            acc = carry[hh][1] + jnp.dot(w[hh], v_ref[pl.ds(start, tq), lanes[hh]],
                                         preferred_element_type=F32)
            run = carry[hh][0] + rest[hh][:, 0:1] + sp[hh][:, 0:1]
            out.append((run, acc))
        return tuple(out)

    def live(carry):
        low = carry[0][0]
        for hh in range(1, hps):
            low = jnp.minimum(low, carry[hh][0])
        return jnp.min(low) < SB_DEAD_LOG

    init = tuple((jnp.zeros((tq, 1), F32), jnp.zeros((tq, d), F32)) for _ in range(hps))
    carry = block(qi, init, True)

    def step(state):
        t, c, _ = state
        c = block(qi - 1 - t, c, False)
        return t + 1, c, live(c)

    _, carry, _ = lax.while_loop(lambda s: jnp.logical_and(s[0] < qi, s[2]), step,
                                 (jnp.int32(0), carry, live(carry)))
    for hh in range(hps):
        out_ref[:, hh * d:(hh + 1) * d] = carry[hh][1].astype(out_ref.dtype)


def _stick_breaking(proj, col0, *, tq, hps):
    b, s, _ = proj.shape
    w = hps * SB_HEAD_DIM
    groups = SB_HEADS // hps
    c0 = col0 // w
    return pl.pallas_call(
        functools.partial(_sb_kernel, tq=tq, hps=hps),
        grid=(b, groups, s // tq),
        in_specs=[
            pl.BlockSpec((None, tq, w), lambda bi, hi, qi: (bi, qi, c0 + hi)),
            pl.BlockSpec((None, s, w), lambda bi, hi, qi: (bi, 0, c0 + groups + hi)),
            pl.BlockSpec((None, s, w), lambda bi, hi, qi: (bi, 0, c0 + 2 * groups + hi)),
        ],
        out_specs=pl.BlockSpec((None, tq, w), lambda bi, hi, qi: (bi, qi, hi)),
        out_shape=jax.ShapeDtypeStruct((b, s, SB_WIDTH), BF16),
        compiler_params=_params("parallel", "parallel", "arbitrary"),
        name="stick_breaking",
    )(proj, proj, proj)


def _proj2_res_kernel(x_ref, a_ref, b_ref, wa_ref, wb_ref, out_ref):
    out_ref[...] = (x_ref[...]
                    + jnp.dot(a_ref[...], wa_ref[...], preferred_element_type=F32)
                    + jnp.dot(b_ref[...], wb_ref[...], preferred_element_type=F32))


def _proj2_res(x, a, b, w, *, tm, tn):
    m, n = x.shape
    ka, kb = a.shape[1], b.shape[1]
    assert ka == kb and w.shape[0] == ka + kb
    return pl.pallas_call(
        _proj2_res_kernel,
        grid=(m // tm, n // tn),
        in_specs=[
            pl.BlockSpec((tm, tn), lambda i, j: (i, j)),
            pl.BlockSpec((tm, ka), lambda i, j: (i, 0)),
            pl.BlockSpec((tm, kb), lambda i, j: (i, 0)),
            pl.BlockSpec((ka, tn), lambda i, j: (0, j)),
            pl.BlockSpec((kb, tn), lambda i, j: (1, j)),
        ],
        out_specs=pl.BlockSpec((tm, tn), lambda i, j: (i, j)),
        out_shape=jax.ShapeDtypeStruct((m, n), F32),
        compiler_params=_params("parallel", "arbitrary"),
        name="out_proj",
    )(x, a, b, w, w)


def _proj_res_kernel(x_ref, a_ref, w_ref, out_ref):
    out_ref[...] = x_ref[...] + jnp.dot(a_ref[...], w_ref[...], preferred_element_type=F32)


def _proj_res(x, a, w, *, tm, tn):
    m, n = x.shape
    k = a.shape[1]
    return pl.pallas_call(
        _proj_res_kernel,
        grid=(m // tm, n // tn),
        in_specs=[
            pl.BlockSpec((tm, tn), lambda i, j: (i, j)),
            pl.BlockSpec((tm, k), lambda i, j: (i, 0)),
            pl.BlockSpec((k, tn), lambda i, j: (0, j)),
        ],
        out_specs=pl.BlockSpec((tm, tn), lambda i, j: (i, j)),
        out_shape=jax.ShapeDtypeStruct((m, n), F32),
        compiler_params=_params("parallel", "arbitrary"),
        name="pw2",
    )(x, a, w)


def _swiglu_kernel(x_ref, g_ref, w1_ref, w3_ref, w2_ref, out_ref, xn_ref, acc_ref):
    j = pl.program_id(1)

    @pl.when(j == 0)
    def _():
        xn_ref[...] = _rms_norm(x_ref[...], g_ref[...]).astype(BF16)
        acc_ref[...] = jnp.zeros_like(acc_ref)

    xn = xn_ref[...]
    a = jnp.dot(xn, w1_ref[...], preferred_element_type=F32)
    b = jnp.dot(xn, w3_ref[...], preferred_element_type=F32)
    h = (a * _sigmoid(a) * b).astype(BF16)
    acc_ref[...] += jnp.dot(h, w2_ref[...], preferred_element_type=F32)

    @pl.when(j == pl.num_programs(1) - 1)
    def _():
        out_ref[...] = x_ref[...] + acc_ref[...]


def _swiglu_res(x, g, w1, w3, w2, *, tm, tf):
    m, d = x.shape
    f = w1.shape[1]
    return pl.pallas_call(
        _swiglu_kernel,
        grid=(m // tm, f // tf),
        in_specs=[
            pl.BlockSpec((tm, d), lambda i, j: (i, 0)),
            pl.BlockSpec((1, d), lambda i, j: (0, 0)),
            pl.BlockSpec((d, tf), lambda i, j: (0, j)),
            pl.BlockSpec((d, tf), lambda i, j: (0, j)),
            pl.BlockSpec((tf, d), lambda i, j: (j, 0)),
        ],
        out_specs=pl.BlockSpec((tm, d), lambda i, j: (i, 0)),
        out_shape=jax.ShapeDtypeStruct((m, d), F32),
        scratch_shapes=[pltpu.VMEM((tm, d), BF16), pltpu.VMEM((tm, d), F32)],
        compiler_params=_params("parallel", "arbitrary"),
        name="swiglu",
    )(x, g, w1, w3, w2)


def _pw1_glu_kernel(x_ref, g_ref, wa_ref, wg_ref, out_ref, xn_ref):
    @pl.when(pl.program_id(1) == 0)
    def _():
        xn_ref[...] = _rms_norm(x_ref[...], g_ref[...]).astype(BF16)

    xn = xn_ref[...]
    a = jnp.dot(xn, wa_ref[...], preferred_element_type=F32)
    gate = jnp.dot(xn, wg_ref[...], preferred_element_type=F32)
    out_ref[...] = (a * _sigmoid(gate)).astype(out_ref.dtype)


def _pw1_glu(x, g, w, *, tm, tn):
    m, d = x.shape
    n = w.shape[1] // 2
    gate0 = n // tn
    return pl.pallas_call(
        _pw1_glu_kernel,
        grid=(m // tm, n // tn),
        in_specs=[
            pl.BlockSpec((tm, d), lambda i, j: (i, 0)),
            pl.BlockSpec((1, d), lambda i, j: (0, 0)),
            pl.BlockSpec((d, tn), lambda i, j: (0, j)),
            pl.BlockSpec((d, tn), lambda i, j: (0, gate0 + j)),
        ],
        out_specs=pl.BlockSpec((tm, tn), lambda i, j: (i, j)),
        out_shape=jax.ShapeDtypeStruct((m, n), BF16),
        scratch_shapes=[pltpu.VMEM((tm, d), BF16)],
        compiler_params=_params("parallel", "arbitrary"),
        name="pw1_glu",
    )(x, g, w, w)


CONV_HALO = 32
CONV_ROWS = 64


def _conv_ln_kernel(halo_ref, u_ref, w_ref, g_ref, b_ref, out_ref, xx_ref, y_ref, *, ts):
    ch = u_ref.shape[1]
    first = pl.program_id(1) == 0
    halo = halo_ref[...].astype(F32)
    xx_ref[pl.ds(0, CONV_HALO), :] = jnp.where(first, 0.0, halo)
    xx_ref[pl.ds(CONV_HALO, ts), :] = u_ref[...].astype(F32)

    span = CONV_ROWS + CONV_HALO
    first_tap = CONV_HALO - (CONV_WIDTH - 1)

    def lane_block(cb, carry):
        c0 = pl.multiple_of(cb * LANES, LANES)
        w = w_ref[:, pl.ds(c0, LANES)]
        for rb in range(ts // CONV_ROWS):
            window = xx_ref[pl.ds(rb * CONV_ROWS, span), pl.ds(c0, LANES)]
            acc = jnp.zeros((CONV_ROWS, LANES), F32)
            for r in range(F32_SUBLANES):
                shifted = window if r == 0 else pltpu.roll(window, span - r, axis=0)
                for off in range(r, CONV_HALO + 1, F32_SUBLANES):
                    j = off - first_tap
                    if 0 <= j < CONV_WIDTH:
                        acc = acc + w[j:j + 1, :] * shifted[off - r:off - r + CONV_ROWS, :]
            y_ref[pl.ds(rb * CONV_ROWS, CONV_ROWS), pl.ds(c0, LANES)] = acc
        return carry

    lax.fori_loop(0, ch // LANES, lane_block, 0)

    y = y_ref[...]
    mu = jnp.mean(y, axis=-1, keepdims=True)
    yc = y - mu
    yn = yc * lax.rsqrt(jnp.mean(yc * yc, axis=-1, keepdims=True) + EPS) * g_ref[...] + b_ref[...]
    out_ref[...] = (yn * _sigmoid(yn)).astype(out_ref.dtype)


def _conv_ln(u, w, g, b, *, ts):
    bsz, s, ch = u.shape
    per = ts // CONV_HALO
    return pl.pallas_call(
        functools.partial(_conv_ln_kernel, ts=ts),
        grid=(bsz, s // ts),
        in_specs=[
            pl.BlockSpec((None, CONV_HALO, ch), lambda bi, i: (bi, jnp.maximum(i * per - 1, 0), 0)),
            pl.BlockSpec((None, ts, ch), lambda bi, i: (bi, i, 0)),
            pl.BlockSpec((CONV_WIDTH, ch), lambda bi, i: (0, 0)),
            pl.BlockSpec((1, ch), lambda bi, i: (0, 0)),
            pl.BlockSpec((1, ch), lambda bi, i: (0, 0)),
        ],
        out_specs=pl.BlockSpec((None, ts, ch), lambda bi, i: (bi, i, 0)),
        out_shape=jax.ShapeDtypeStruct((bsz, s, ch), BF16),
        scratch_shapes=[pltpu.VMEM((CONV_HALO + ts, ch), F32), pltpu.VMEM((ts, ch), F32)],
        compiler_params=_params("parallel", "parallel"),
        name="conv_ln",
    )(u, u, w, g, b)


def _router_kernel(x_ref, g_ref, wr_hi_ref, wr_lo_ref, br_ref, xn_ref, idx_ref, gate_ref):
    xn = _rms_norm(x_ref[...], g_ref[...])
    xn_ref[...] = xn
    xn_hi, xn_lo = _split_bf16(xn)
    logits = _dot_3pass(xn_hi, xn_lo, wr_hi_ref, wr_lo_ref) + br_ref[...]
    lane = lax.broadcasted_iota(jnp.int32, logits.shape, 1)
    logits = jnp.where(lane < N_EXPERTS, logits, -jnp.inf)
    m1 = jnp.max(logits, axis=1, keepdims=True)
    i1 = jnp.min(jnp.where(logits == m1, lane, LANES), axis=1, keepdims=True)
    rest = jnp.where(lane == i1, -jnp.inf, logits)
    m2 = jnp.max(rest, axis=1, keepdims=True)
    i2 = jnp.min(jnp.where(rest == m2, lane, LANES), axis=1, keepdims=True)
    e2 = jnp.exp(m2 - m1)
    g1 = 1.0 / (1.0 + e2)
    g2 = e2 / (1.0 + e2)
    idx_ref[...] = jnp.where(lane == 0, i1, i2)
    gate_ref[...] = jnp.where(lane == 0, g1, g2)


def _router(x, g, wr, br, *, tm):
    m, d = x.shape
    wr_hi, wr_lo = _split_bf16(wr)
    return pl.pallas_call(
        _router_kernel,
        grid=(m // tm,),
        in_specs=[
            pl.BlockSpec((tm, d), lambda i: (i, 0)),
            pl.BlockSpec((1, d), lambda i: (0, 0)),
            pl.BlockSpec((d, LANES), lambda i: (0, 0)),
            pl.BlockSpec((d, LANES), lambda i: (0, 0)),
            pl.BlockSpec((1, LANES), lambda i: (0, 0)),
        ],
        out_specs=[
            pl.BlockSpec((tm, d), lambda i: (i, 0)),
            pl.BlockSpec((tm, LANES), lambda i: (i, 0)),
            pl.BlockSpec((tm, LANES), lambda i: (i, 0)),
        ],
        out_shape=[
            jax.ShapeDtypeStruct((m, d), F32),
            jax.ShapeDtypeStruct((m, LANES), jnp.int32),
            jax.ShapeDtypeStruct((m, LANES), F32),
        ],
        compiler_params=_params("parallel"),
        name="router",
    )(x, g, wr_hi, wr_lo, br)


def _row_copy(src_hbm, src_row, dst_ref, dst_row, sem):
    return pltpu.make_async_copy(src_hbm.at[pl.ds(src_row, 1)], dst_ref.at[pl.ds(dst_row, 1)], sem)


def _block_wait(src_hbm, dst_ref, sem):
    pltpu.make_async_copy(src_hbm.at[pl.ds(0, dst_ref.shape[0])], dst_ref, sem).wait()


MOE_GATHER_ROWS = 64
MOE_BLOCK_FRACTIONS = (4, 2, 1)


def _moe_kernel(be_ref, nu_ref, rows_ref, tok_ref, x_hbm, w1_ref, w3_ref, w2_ref, out_ref,
                xg_ref, xb_ref, sem, *, tm):
    i = pl.program_id(0)
    j = pl.program_id(1)
    n_used = nu_ref[0]
    used = i < n_used
    rows = rows_ref[i]
    gather_steps = tm // MOE_GATHER_ROWS
    gathering = jnp.logical_and(i + 1 < n_used, jnp.logical_and(j >= 1, j <= gather_steps))

    def gather(block, r0):
        for r in range(MOE_GATHER_ROWS):
            row = r0 + r
            _row_copy(x_hbm, tok_ref[block * tm + row], xg_ref, row, sem).start()

    @pl.when(jnp.logical_and(i == 0, j == 0))
    def _():
        def body(c, carry):
            gather(0, c * MOE_GATHER_ROWS)
            return carry
        lax.fori_loop(0, gather_steps, body, 0)

    @pl.when(jnp.logical_and(used, j == 0))
    def _():
        _block_wait(x_hbm, xg_ref, sem)
        xb_ref[...] = xg_ref[...].astype(BF16)

    @pl.when(j == 0)
    def _():
        out_ref[...] = jnp.zeros_like(out_ref)

    def compute(m, with_gather):
        if with_gather:
            gather(i + 1, (j - 1) * MOE_GATHER_ROWS)
        xb = xb_ref[pl.ds(0, m), :]
        a = jnp.dot(xb, w1_ref[...].astype(BF16), preferred_element_type=F32)
        b = jnp.dot(xb, w3_ref[...].astype(BF16), preferred_element_type=F32)
        h = (a * _sigmoid(a) * b).astype(BF16)
        out_ref[pl.ds(0, m), :] += jnp.dot(h, w2_ref[...].astype(BF16), preferred_element_type=F32)

    lower = 0
    for m in MOE_BLOCK_FRACTIONS:
        rows_m = tm // m
        fits = jnp.logical_and(rows > lower, rows <= rows_m)
        lower = rows_m
        for with_gather in (True, False):
            cond = jnp.logical_and(fits, gathering == with_gather)
            pl.when(cond)(functools.partial(compute, rows_m, with_gather))


def _moe_experts(block_e, n_used, block_rows, row_tok, x, w1, w3, w2, *, tm, tf):
    n_rows = row_tok.shape[0]
    d = x.shape[1]
    f = w1.shape[2]
    nf = f // tf
    assert tm % MOE_GATHER_ROWS == 0 and nf > tm // MOE_GATHER_ROWS

    def ff(i, j, nu):
        return jnp.where(i < nu[0], j, nf - 1)

    return pl.pallas_call(
        functools.partial(_moe_kernel, tm=tm),
        grid_spec=pltpu.PrefetchScalarGridSpec(
            num_scalar_prefetch=4,
            grid=(n_rows // tm, nf),
            in_specs=[
                pl.BlockSpec(memory_space=pl.ANY),
                pl.BlockSpec((None, d, tf), lambda i, j, be, nu, rows, tok: (be[i], 0, ff(i, j, nu))),
                pl.BlockSpec((None, d, tf), lambda i, j, be, nu, rows, tok: (be[i], 0, ff(i, j, nu))),
                pl.BlockSpec((None, tf, d), lambda i, j, be, nu, rows, tok: (be[i], ff(i, j, nu), 0)),
            ],
            out_specs=pl.BlockSpec((tm, d), lambda i, j, be, nu, rows, tok: (i, 0)),
            scratch_shapes=[pltpu.VMEM((tm, d), F32), pltpu.VMEM((tm, d), BF16),
                            pltpu.SemaphoreType.DMA(())],
        ),
        out_shape=jax.ShapeDtypeStruct((n_rows, d), F32),
        compiler_params=_params("arbitrary", "arbitrary"),
        name="moe_experts",
    )(block_e, n_used, block_rows, row_tok, x, w1, w3, w2)


COMBINE_UNROLL = 8


def _combine_kernel(dest_ref, dest_next_ref, y_hbm, h_ref, gate_ref, fn_ref, out_ref,
                    buf_ref, sem, *, tc):
    i = pl.program_id(0)
    slot = i % 2

    def fetch(idx_ref, dst_slot):
        def body(r, carry):
            for k in range(TOP_K):
                _row_copy(y_hbm, idx_ref[TOP_K * r + k], buf_ref.at[dst_slot, k], r,
                          sem.at[dst_slot]).start()
            return carry
        lax.fori_loop(0, tc, body, 0, unroll=COMBINE_UNROLL)

    @pl.when(i == 0)
    def _():
        fetch(dest_ref, 0)

    @pl.when(i + 1 < pl.num_programs(0))
    def _():
        fetch(dest_next_ref, 1 - slot)

    for k in range(TOP_K):
        _block_wait(y_hbm, buf_ref.at[slot, k], sem.at[slot])
    gates = gate_ref[...]
    y = h_ref[...] + gates[:, 0:1] * buf_ref[slot, 0] + gates[:, 1:2] * buf_ref[slot, 1]
    out_ref[...] = _rms_norm(y, fn_ref[...])


def _combine(dest, y_rows, h, gates, fn, *, tc):
    m, d = h.shape
    last = m // tc - 1
    return pl.pallas_call(
        functools.partial(_combine_kernel, tc=tc),
        grid=(m // tc,),
        in_specs=[
            pl.BlockSpec((TOP_K * tc,), lambda i: (i,), memory_space=pltpu.SMEM),
            pl.BlockSpec((TOP_K * tc,), lambda i: (jnp.minimum(i + 1, last),),
                         memory_space=pltpu.SMEM),
            pl.BlockSpec(memory_space=pl.ANY),
            pl.BlockSpec((tc, d), lambda i: (i, 0)),
            pl.BlockSpec((tc, LANES), lambda i: (i, 0)),
            pl.BlockSpec((1, d), lambda i: (0, 0)),
        ],
        out_specs=pl.BlockSpec((tc, d), lambda i: (i, 0)),
        out_shape=jax.ShapeDtypeStruct((m, d), F32),
        scratch_shapes=[pltpu.VMEM((2, TOP_K, tc, d), F32), pltpu.SemaphoreType.DMA((2,))],
        compiler_params=_params("arbitrary"),
        name="moe_combine",
    )(dest, dest, y_rows, h, gates, fn)


def _dispatch_plan(top_idx, n_tok, tm):
    flat_e = top_idx.reshape(-1)
    n_asg = flat_e.shape[0]
    onehot = (flat_e[:, None] == jnp.arange(N_EXPERTS, dtype=jnp.int32)[None, :]).astype(jnp.int32)
    csum = jnp.cumsum(onehot, axis=0)
    rank = jnp.sum(onehot * csum, axis=1) - 1
    counts = csum[-1]
    padded = ((counts + tm - 1) // tm) * tm
    pad_end = jnp.cumsum(padded)
    pad_start = pad_end - padded
    dest = (pad_start[flat_e] + rank).astype(jnp.int32)
    n_blocks = n_asg // tm + N_EXPERTS
    n_used = (pad_end[-1] // tm).astype(jnp.int32)
    flat_tok = jnp.arange(n_asg, dtype=jnp.int32) // TOP_K
    row_tok = jnp.zeros((n_blocks * tm,), jnp.int32).at[dest].set(flat_tok)
    blk = jnp.minimum(jnp.arange(n_blocks, dtype=jnp.int32), n_used - 1) * tm
    block_e = jnp.minimum(jnp.searchsorted(pad_end, blk, side="right"), N_EXPERTS - 1).astype(jnp.int32)
    first = jnp.arange(n_blocks, dtype=jnp.int32) * tm
    block_rows = jnp.clip(pad_start[block_e] + counts[block_e] - first, 0, tm).astype(jnp.int32)
    return dest, row_tok, block_e, block_rows, n_used.reshape(1)


def _pad_lanes(a):
    return jnp.pad(a, [(0, 0)] * (a.ndim - 1) + [(0, LANES - a.shape[-1])])


def kernel(x, ev_norm_mix, ev_w_in, ev_conv_qk, ev_b_gates, ev_ml_norm, ev_w_out, ev_norm_ffn,
           ev_w1, ev_w3, ev_w2, od_norm_mix, od_pw1, od_dw, od_ln_g, od_ln_b, od_pw2,
           od_norm_ffn, od_router, od_router_b, od_moe_w1, od_moe_w3, od_moe_w2, final_norm):
    bsz, s, d = x.shape
    m = bsz * s
    h = x.reshape(m, d)

    w_in = ev_w_in[0]
    gate0 = 4 * ML_WIDTH
    gate1 = gate0 + 2 * ML_HEADS
    w_main = jnp.concatenate([w_in[:, :gate0], w_in[:, gate1:]], axis=1).astype(BF16)
    w_gates = _pad_lanes(w_in[:, gate0:gate1])
    b_gates = _pad_lanes(ev_b_gates[0][None, :])
    proj, gates = _in_proj(h, ev_norm_mix[0][None, :], w_main, w_gates, b_gates, tm=1024, tn=1024)
    proj = proj.reshape(bsz, s, -1)
    n_chunks = s // ML_CHUNK
    gates = gates[:, :2 * ML_HEADS].reshape(bsz, s, 2 * ML_HEADS).transpose(0, 2, 1)
    i_pre = gates[:, :ML_HEADS].reshape(bsz, ML_HEADS, n_chunks, ML_CHUNK)
    f_pre = gates[:, ML_HEADS:].reshape(bsz, ML_HEADS, n_chunks, ML_CHUNK)
    h_ml = _mlstm(proj, ev_conv_qk[0], i_pre, f_pre, ev_ml_norm[0][None, :], hps=2)
    h_sb = _stick_breaking(proj, gate0, tq=256, hps=8)
    h = _proj2_res(h, h_ml.reshape(m, ML_WIDTH), h_sb.reshape(m, SB_WIDTH),
                   ev_w_out[0].astype(BF16), tm=1024, tn=1024)

    h = _swiglu_res(h, ev_norm_ffn[0][None, :], ev_w1[0].astype(BF16), ev_w3[0].astype(BF16),
                    ev_w2[0].astype(BF16), tm=512, tf=512)

    cw = od_pw1.shape[2] // 2
    u = _pw1_glu(h, od_norm_mix[0][None, :], od_pw1[0].astype(BF16), tm=1024, tn=512)
    u = _conv_ln(u.reshape(bsz, s, cw), od_dw[0], od_ln_g[0][None, :], od_ln_b[0][None, :], ts=256)
    h = _proj_res(h, u.reshape(m, cw), od_pw2[0].astype(BF16), tm=1024, tn=1024)

    tm_e = 1024
    xn, top_idx, top_gate = _router(h, od_norm_ffn[0][None, :], _pad_lanes(od_router[0]),
                                    _pad_lanes(od_router_b[0][None, :]), tm=512)
    dest, row_tok, block_e, block_rows, n_used = _dispatch_plan(top_idx[:, :TOP_K], m, tm_e)
    y_rows = _moe_experts(block_e, n_used, block_rows, row_tok, xn,
                          od_moe_w1[0], od_moe_w3[0], od_moe_w2[0],
                          tm=tm_e, tf=256)
    out = _combine(dest, y_rows, h, top_gate, final_norm[None, :], tc=256)
    return out.reshape(bsz, s, d)
```

```python
import functools

import jax
import jax.numpy as jnp
from jax import lax
from jax.experimental import pallas as pl
from jax.experimental.pallas import tpu as pltpu

ML_HEADS = 4
ML_HEAD_DIM = 256
ML_WIDTH = ML_HEADS * ML_HEAD_DIM
SB_HEADS = 8
SB_HEAD_DIM = 128
SB_WIDTH = SB_HEADS * SB_HEAD_DIM
QK_CONV = 4
ML_CHUNK = 128
CONV_WIDTH = 31
N_EXPERTS = 8
TOP_K = 2
EPS = 1e-6

BF16 = jnp.bfloat16
F32 = jnp.float32

LANES = 128
F32_SUBLANES = 8
BF16_SUBLANES = 16
V7X_VMEM_LIMIT = 56 * 1024 * 1024

NT_DIMS = (((1,), (1,)), ((), ()))
TN_DIMS = (((0,), (0,)), ((), ()))


def _params(*semantics):
    return pltpu.CompilerParams(dimension_semantics=semantics, vmem_limit_bytes=V7X_VMEM_LIMIT)


def _rms_norm(x, g):
    return x * lax.rsqrt(jnp.mean(x * x, axis=-1, keepdims=True) + EPS) * g


def _sigmoid(x):
    return 1.0 / (1.0 + jnp.exp(-x))


def _softplus(z):
    return jnp.maximum(z, 0.0) + jnp.log(1.0 + jnp.exp(-jnp.abs(z)))


def _split_bf16(a):
    hi = a.astype(BF16)
    lo = (a - hi.astype(F32)).astype(BF16)
    return hi, lo


def _dot_3pass(x_hi, x_lo, w_hi_ref, w_lo_ref):
    w_hi = w_hi_ref[...]
    return (jnp.dot(x_hi, w_hi, preferred_element_type=F32)
            + jnp.dot(x_lo, w_hi, preferred_element_type=F32)
            + jnp.dot(x_hi, w_lo_ref[...], preferred_element_type=F32))


def _in_proj_kernel(x_ref, g_ref, w_ref, wg_hi_ref, wg_lo_ref, bg_ref, proj_ref, gates_ref, xn_ref):
    @pl.when(pl.program_id(1) == 0)
    def _():
        xn_hi, xn_lo = _split_bf16(_rms_norm(x_ref[...], g_ref[...]))
        xn_ref[...] = xn_hi
        gates_ref[...] = _dot_3pass(xn_hi, xn_lo, wg_hi_ref, wg_lo_ref) + bg_ref[...]

    proj_ref[...] = jnp.dot(xn_ref[...], w_ref[...], preferred_element_type=F32).astype(BF16)


def _in_proj(x, g, w, wg, bg, *, tm):
    m, d = x.shape
    tiles, _, tn = w.shape
    n = tiles * tn
    wg_hi, wg_lo = _split_bf16(wg)
    return pl.pallas_call(
        _in_proj_kernel,
        grid=(m // tm, tiles),
        in_specs=[
            pl.BlockSpec((tm, d), lambda i, j: (i, 0)),
            pl.BlockSpec((1, d), lambda i, j: (0, 0)),
            pl.BlockSpec((None, d, tn), lambda i, j: (j, 0, 0)),
            pl.BlockSpec((d, LANES), lambda i, j: (0, 0)),
            pl.BlockSpec((d, LANES), lambda i, j: (0, 0)),
            pl.BlockSpec((1, LANES), lambda i, j: (0, 0)),
        ],
        out_specs=[
            pl.BlockSpec((tm, tn), lambda i, j: (i, j)),
            pl.BlockSpec((tm, LANES), lambda i, j: (i, 0)),
        ],
        out_shape=[jax.ShapeDtypeStruct((m, n), BF16), jax.ShapeDtypeStruct((m, LANES), F32)],
        scratch_shapes=[pltpu.VMEM((tm, d), BF16)],
        compiler_params=_params("parallel", "arbitrary"),
        name="in_proj",
    )(x, g, w, wg_hi, wg_lo, bg)


def _mlstm_kernel(q_ref, k_ref, v_ref, o_ref, cq_ref, ck_ref, i_ref, f_ref, nrm_ref,
                  out_ref, c_state, n_state, m_state, *, hps):
    L = ML_CHUNK
    d = ML_HEAD_DIM
    n_chunks = q_ref.shape[0] // L
    halo = BF16_SUBLANES

    c_state[...] = jnp.zeros_like(c_state)
    n_state[...] = jnp.zeros_like(n_state)
    m_state[...] = jnp.zeros_like(m_state)

    row = lax.broadcasted_iota(jnp.int32, (L, L), 0)
    col = lax.broadcasted_iota(jnp.int32, (L, L), 1)
    causal = col <= row
    eye = col == row

    def conv_swish(x_ref, w_ref, c, lanes):
        start = pl.multiple_of(c * L, L)
        cur = x_ref[pl.ds(start, L), lanes].astype(F32)
        prev_start = pl.multiple_of(jnp.maximum(start - halo, 0), halo)
        prev = x_ref[pl.ds(prev_start, halo), lanes].astype(F32)
        prev = jnp.where(c > 0, prev, 0.0)
        xx = jnp.concatenate([prev, cur], axis=0)
        w = w_ref[:, lanes]
        y = jnp.zeros((L, d), F32)
        for j in range(QK_CONV):
            off = halo - (QK_CONV - 1) + j
            r = off % F32_SUBLANES
            shifted = xx if r == 0 else pltpu.roll(xx, halo + L - r, axis=0)
            y = y + w[j:j + 1, :] * shifted[off - r:off - r + L, :]
        return y * _sigmoid(y)

    def to_col(r):
        return jnp.sum(jnp.where(eye, r, 0.0), axis=1, keepdims=True)

    def head_chunk(hh, c):
        lanes = slice(hh * d, (hh + 1) * d)
        start = pl.multiple_of(c * L, L)
        q = conv_swish(q_ref, cq_ref, c, lanes)
        k = conv_swish(k_ref, ck_ref, c, lanes) * (d ** -0.5)
        v = v_ref[pl.ds(start, L), lanes]
        li_row = i_ref[hh, pl.ds(c, 1), :]
        f_row = f_ref[hh, pl.ds(c, 1), :]
        lf_row = -_softplus(-f_row)
        lf_col = to_col(lf_row)
        li_col = to_col(li_row)
        bl_col = jnp.sum(jnp.where(causal, lf_row, 0.0), axis=1, keepdims=True)
        bl_row = jnp.sum(jnp.where(col >= row, lf_col, 0.0), axis=0, keepdims=True)
        g = jnp.sum(lf_row, axis=1, keepdims=True)

        c_prev = c_state[hh]
        n_prev = n_state[hh]
        m_prev = m_state[hh]

        d_log = jnp.where(causal, bl_col - bl_row + li_row, -jnp.inf)
        inter = bl_col + m_prev
        m_t = jnp.maximum(inter, jnp.max(d_log, axis=1, keepdims=True))
        w_intra = jnp.exp(d_log - m_t)
        w_inter = jnp.exp(inter - m_t)
        q_b = q.astype(BF16)
        k_b = k.astype(BF16)
        s_qk = lax.dot_general(q_b, k_b, NT_DIMS, preferred_element_type=F32) * w_intra
        num = (w_inter * jnp.dot(q_b, c_prev.astype(BF16), preferred_element_type=F32)
               + jnp.dot(s_qk.astype(BF16), v, preferred_element_type=F32))
        qn = (w_inter * jnp.sum(q * n_prev, axis=1, keepdims=True)
              + jnp.sum(s_qk, axis=1, keepdims=True))
        denom = jnp.maximum(jnp.abs(qn), jnp.exp(-m_t))
        h = num / denom
        h = h * lax.rsqrt(jnp.mean(h * h, axis=1, keepdims=True) + EPS)
        h = h * nrm_ref[:, lanes] * _sigmoid(o_ref[pl.ds(start, L), lanes].astype(F32))
        out_ref[pl.ds(start, L), lanes] = h.astype(out_ref.dtype)

        a_col = g - bl_col + li_col
        m_loc = jnp.max(a_col, axis=0, keepdims=True)
        kw = k * jnp.exp(a_col - m_loc)
        c_loc = lax.dot_general(kw.astype(BF16), v, TN_DIMS, preferred_element_type=F32)
        n_loc = jnp.sum(kw, axis=0, keepdims=True)
        m_new = jnp.maximum(g + m_prev, m_loc)
        s_prev = jnp.exp(g + m_prev - m_new)
        s_loc = jnp.exp(m_loc - m_new)
        c_state[hh] = s_prev * c_prev + s_loc * c_loc
        n_state[hh] = s_prev * n_prev + s_loc * n_loc
        m_state[hh] = m_new

    def chunk(c, carry):
        for hh in range(hps):
            head_chunk(hh, c)
        return carry

    lax.fori_loop(0, n_chunks, chunk, 0)


def _mlstm(proj, conv_qk, i_pre, f_pre, ml_norm, *, hps):
    b, s, _ = proj.shape
    d = ML_HEAD_DIM
    w = hps * d
    groups = ML_HEADS // hps
    n_chunks = s // ML_CHUNK

    def col_spec(part):
        return pl.BlockSpec((None, s, w), lambda bi, hi: (bi, 0, part * groups + hi))

    gate_spec = pl.BlockSpec((None, hps, n_chunks, ML_CHUNK), lambda bi, hi: (bi, hi, 0, 0))
    return pl.pallas_call(
        functools.partial(_mlstm_kernel, hps=hps),
        grid=(b, groups),
        in_specs=[
            col_spec(0), col_spec(1), col_spec(2), col_spec(3),
            pl.BlockSpec((QK_CONV, w), lambda bi, hi: (0, hi)),
            pl.BlockSpec((QK_CONV, w), lambda bi, hi: (0, groups + hi)),
            gate_spec, gate_spec,
            pl.BlockSpec((1, w), lambda bi, hi: (0, hi)),
        ],
        out_specs=pl.BlockSpec((None, s, w), lambda bi, hi: (bi, 0, hi)),
        out_shape=jax.ShapeDtypeStruct((b, s, ML_WIDTH), BF16),
        scratch_shapes=[pltpu.VMEM((hps, d, d), F32), pltpu.VMEM((hps, 1, d), F32),
                        pltpu.VMEM((hps, 1, 1), F32)],
        compiler_params=_params("parallel", "parallel"),
        name="mlstm",
    )(proj, proj, proj, proj, conv_qk, conv_qk, i_pre, f_pre, ml_norm)


SB_DEAD_LOG = 104.0


def _sb_kernel(q_ref, k_ref, v_ref, out_ref, *, tq, hps):
    qi = pl.program_id(2)
    d = SB_HEAD_DIM
    scale = d ** -0.5
    row = lax.broadcasted_iota(jnp.int32, (tq, tq), 0)
    col = lax.broadcasted_iota(jnp.int32, (tq, tq), 1)
    strict = col < row
    suffix = (row > col).astype(BF16)

    def block(kb, carry, diag):
        start = pl.multiple_of(kb * tq, tq)
        heads = range(hps)
        lanes = [slice(hh * d, (hh + 1) * d) for hh in heads]
        z = [lax.dot_general(q_ref[:, lanes[hh]], k_ref[pl.ds(start, tq), lanes[hh]], NT_DIMS,
                             preferred_element_type=F32) * scale for hh in heads]
        sp, log_beta, sp_hi, sp_lo = [], [], [], []
        for hh in heads:
            s = _softplus(z[hh])
            log_beta.append(z[hh] - s)
            if diag:
                s = jnp.where(strict, s, 0.0)
            hi = s.astype(BF16)
            sp.append(s)
            sp_hi.append(hi)
            sp_lo.append((s - hi.astype(F32)).astype(BF16))
        rest = [jnp.dot(sp_hi[hh], suffix, preferred_element_type=F32)
                + jnp.dot(sp_lo[hh], suffix, preferred_element_type=F32) for hh in heads]
        w = []
        for hh in heads:
            wh = jnp.exp(log_beta[hh] - rest[hh] - carry[hh][0])
            if diag:
                wh = jnp.where(strict, wh, 0.0)
            w.append(wh.astype(BF16))
        out = []
        for hh in heads:
            acc = carry[hh][1] + jnp.dot(w[hh], v_ref[pl.ds(start, tq), lanes[hh]],
                                         preferred_element_type=F32)
            run = carry[hh][0] + rest[hh][:, 0:1] + sp[hh][:, 0:1]
            out.append((run, acc))
        return tuple(out)

    def live(carry):
        low = carry[0][0]
        for hh in range(1, hps):
            low = jnp.minimum(low, carry[hh][0])
        return jnp.min(low) < SB_DEAD_LOG

    init = tuple((jnp.zeros((tq, 1), F32), jnp.zeros((tq, d), F32)) for _ in range(hps))
    carry = block(qi, init, True)

    def step(state):
        t, c, _ = state
        c = block(qi - 1 - t, c, False)
        return t + 1, c, live(c)

    _, carry, _ = lax.while_loop(lambda s: jnp.logical_and(s[0] < qi, s[2]), step,
                                 (jnp.int32(0), carry, live(carry)))
    for hh in range(hps):
        out_ref[:, hh * d:(hh + 1) * d] = carry[hh][1].astype(out_ref.dtype)


def _stick_breaking(proj, col0, *, tq, hps):
    b, s, _ = proj.shape
    w = hps * SB_HEAD_DIM
    groups = SB_HEADS // hps
    c0 = col0 // w
    return pl.pallas_call(
        functools.partial(_sb_kernel, tq=tq, hps=hps),
        grid=(b, groups, s // tq),
        in_specs=[
            pl.BlockSpec((None, tq, w), lambda bi, hi, qi: (bi, qi, c0 + hi)),
            pl.BlockSpec((None, s, w), lambda bi, hi, qi: (bi, 0, c0 + groups + hi)),
            pl.BlockSpec((None, s, w), lambda bi, hi, qi: (bi, 0, c0 + 2 * groups + hi)),
        ],
        out_specs=pl.BlockSpec((None, tq, w), lambda bi, hi, qi: (bi, qi, hi)),
        out_shape=jax.ShapeDtypeStruct((b, s, SB_WIDTH), BF16),
        compiler_params=_params("parallel", "parallel", "arbitrary"),
        name="stick_breaking",
    )(proj, proj, proj)


def _proj2_res_kernel(x_ref, a_ref, b_ref, wa_ref, wb_ref, out_ref):
    out_ref[...] = (x_ref[...]
                    + jnp.dot(a_ref[...], wa_ref[...], preferred_element_type=F32)
                    + jnp.dot(b_ref[...], wb_ref[...], preferred_element_type=F32))


def _proj2_res(x, a, b, w, *, tm, tn):
    m, n = x.shape
    ka, kb = a.shape[1], b.shape[1]
    assert ka == kb and w.shape[0] == ka + kb
    return pl.pallas_call(
        _proj2_res_kernel,
        grid=(m // tm, n // tn),
        in_specs=[
            pl.BlockSpec((tm, tn), lambda i, j: (i, j)),
            pl.BlockSpec((tm, ka), lambda i, j: (i, 0)),
            pl.BlockSpec((tm, kb), lambda i, j: (i, 0)),
            pl.BlockSpec((ka, tn), lambda i, j: (0, j)),
            pl.BlockSpec((kb, tn), lambda i, j: (1, j)),
        ],
        out_specs=pl.BlockSpec((tm, tn), lambda i, j: (i, j)),
        out_shape=jax.ShapeDtypeStruct((m, n), F32),
        compiler_params=_params("parallel", "arbitrary"),
        name="out_proj",
    )(x, a, b, w, w)


def _proj_res_kernel(x_ref, a_ref, w_ref, out_ref):
    out_ref[...] = x_ref[...] + jnp.dot(a_ref[...], w_ref[...], preferred_element_type=F32)


def _proj_res(x, a, w, *, tm, tn):
    m, n = x.shape
    k = a.shape[1]
    return pl.pallas_call(
        _proj_res_kernel,
        grid=(m // tm, n // tn),
        in_specs=[
            pl.BlockSpec((tm, tn), lambda i, j: (i, j)),
            pl.BlockSpec((tm, k), lambda i, j: (i, 0)),
            pl.BlockSpec((k, tn), lambda i, j: (0, j)),
        ],
        out_specs=pl.BlockSpec((tm, tn), lambda i, j: (i, j)),
        out_shape=jax.ShapeDtypeStruct((m, n), F32),
        compiler_params=_params("parallel", "arbitrary"),
        name="pw2",
    )(x, a, w)


def _swiglu_kernel(x_ref, g_ref, w1_ref, w3_ref, w2_ref, out_ref, xn_ref, acc_ref):
    j = pl.program_id(1)

    @pl.when(j == 0)
    def _():
        xn_ref[...] = _rms_norm(x_ref[...], g_ref[...]).astype(BF16)
        acc_ref[...] = jnp.zeros_like(acc_ref)

    xn = xn_ref[...]
    a = jnp.dot(xn, w1_ref[...], preferred_element_type=F32)
    b = jnp.dot(xn, w3_ref[...], preferred_element_type=F32)
    h = (a * _sigmoid(a) * b).astype(BF16)
    acc_ref[...] += jnp.dot(h, w2_ref[...], preferred_element_type=F32)

    @pl.when(j == pl.num_programs(1) - 1)
    def _():
        out_ref[...] = x_ref[...] + acc_ref[...]


def _column_tiles(w, tn):
    d, n = w.shape
    return w.reshape(d, n // tn, tn).transpose(1, 0, 2).astype(BF16)


def _swiglu_res(x, g, w1, w3, w2, *, tm):
    m, d = x.shape
    nf, _, tf = w1.shape
    return pl.pallas_call(
        _swiglu_kernel,
        grid=(m // tm, nf),
        in_specs=[
            pl.BlockSpec((tm, d), lambda i, j: (i, 0)),
            pl.BlockSpec((1, d), lambda i, j: (0, 0)),
            pl.BlockSpec((None, d, tf), lambda i, j: (j, 0, 0)),
            pl.BlockSpec((None, d, tf), lambda i, j: (j, 0, 0)),
            pl.BlockSpec((tf, d), lambda i, j: (j, 0)),
        ],
        out_specs=pl.BlockSpec((tm, d), lambda i, j: (i, 0)),
        out_shape=jax.ShapeDtypeStruct((m, d), F32),
        scratch_shapes=[pltpu.VMEM((tm, d), BF16), pltpu.VMEM((tm, d), F32)],
        compiler_params=_params("parallel", "arbitrary"),
        name="swiglu",
    )(x, g, w1, w3, w2)


def _pw1_glu_kernel(x_ref, g_ref, wa_ref, wg_ref, out_ref, xn_ref):
    @pl.when(pl.program_id(1) == 0)
    def _():
        xn_ref[...] = _rms_norm(x_ref[...], g_ref[...]).astype(BF16)

    xn = xn_ref[...]
    a = jnp.dot(xn, wa_ref[...], preferred_element_type=F32)
    gate = jnp.dot(xn, wg_ref[...], preferred_element_type=F32)
    out_ref[...] = (a * _sigmoid(gate)).astype(out_ref.dtype)


def _pw1_glu(x, g, w, *, tm):
    m, d = x.shape
    tiles, _, tn = w.shape
    gate0 = tiles // 2
    n = gate0 * tn
    return pl.pallas_call(
        _pw1_glu_kernel,
        grid=(m // tm, gate0),
        in_specs=[
            pl.BlockSpec((tm, d), lambda i, j: (i, 0)),
            pl.BlockSpec((1, d), lambda i, j: (0, 0)),
            pl.BlockSpec((None, d, tn), lambda i, j: (j, 0, 0)),
            pl.BlockSpec((None, d, tn), lambda i, j: (gate0 + j, 0, 0)),
        ],
        out_specs=pl.BlockSpec((tm, tn), lambda i, j: (i, j)),
        out_shape=jax.ShapeDtypeStruct((m, n), BF16),
        scratch_shapes=[pltpu.VMEM((tm, d), BF16)],
        compiler_params=_params("parallel", "arbitrary"),
        name="pw1_glu",
    )(x, g, w, w)


CONV_HALO = 32
CONV_ROWS = 64


def _conv_ln_kernel(halo_ref, u_ref, w_ref, g_ref, b_ref, out_ref, xx_ref, y_ref, *, ts):
    ch = u_ref.shape[1]
    first = pl.program_id(1) == 0
    halo = halo_ref[...].astype(F32)
    xx_ref[pl.ds(0, CONV_HALO), :] = jnp.where(first, 0.0, halo)
    xx_ref[pl.ds(CONV_HALO, ts), :] = u_ref[...].astype(F32)

    span = CONV_ROWS + CONV_HALO
    first_tap = CONV_HALO - (CONV_WIDTH - 1)

    def lane_block(cb, carry):
        c0 = pl.multiple_of(cb * LANES, LANES)
        w = w_ref[:, pl.ds(c0, LANES)]
        for rb in range(ts // CONV_ROWS):
            window = xx_ref[pl.ds(rb * CONV_ROWS, span), pl.ds(c0, LANES)]
            acc = jnp.zeros((CONV_ROWS, LANES), F32)
            for r in range(F32_SUBLANES):
                shifted = window if r == 0 else pltpu.roll(window, span - r, axis=0)
                for off in range(r, CONV_HALO + 1, F32_SUBLANES):
                    j = off - first_tap
                    if 0 <= j < CONV_WIDTH:
                        acc = acc + w[j:j + 1, :] * shifted[off - r:off - r + CONV_ROWS, :]
            y_ref[pl.ds(rb * CONV_ROWS, CONV_ROWS), pl.ds(c0, LANES)] = acc
        return carry

    lax.fori_loop(0, ch // LANES, lane_block, 0)

    y = y_ref[...]
    mu = jnp.mean(y, axis=-1, keepdims=True)
    yc = y - mu
    yn = yc * lax.rsqrt(jnp.mean(yc * yc, axis=-1, keepdims=True) + EPS) * g_ref[...] + b_ref[...]
    out_ref[...] = (yn * _sigmoid(yn)).astype(out_ref.dtype)


def _conv_ln(u, w, g, b, *, ts):
    bsz, s, ch = u.shape
    per = ts // CONV_HALO
    return pl.pallas_call(
        functools.partial(_conv_ln_kernel, ts=ts),
        grid=(bsz, s // ts),
        in_specs=[
            pl.BlockSpec((None, CONV_HALO, ch), lambda bi, i: (bi, jnp.maximum(i * per - 1, 0), 0)),
            pl.BlockSpec((None, ts, ch), lambda bi, i: (bi, i, 0)),
            pl.BlockSpec((CONV_WIDTH, ch), lambda bi, i: (0, 0)),
            pl.BlockSpec((1, ch), lambda bi, i: (0, 0)),
            pl.BlockSpec((1, ch), lambda bi, i: (0, 0)),
        ],
        out_specs=pl.BlockSpec((None, ts, ch), lambda bi, i: (bi, i, 0)),
        out_shape=jax.ShapeDtypeStruct((bsz, s, ch), BF16),
        scratch_shapes=[pltpu.VMEM((CONV_HALO + ts, ch), F32), pltpu.VMEM((ts, ch), F32)],
        compiler_params=_params("parallel", "parallel"),
        name="conv_ln",
    )(u, u, w, g, b)


def _router_kernel(x_ref, g_ref, wr_hi_ref, wr_lo_ref, br_ref, xn_ref, idx_ref, gate_ref):
    xn = _rms_norm(x_ref[...], g_ref[...])
    xn_ref[...] = xn
    xn_hi, xn_lo = _split_bf16(xn)
    logits = _dot_3pass(xn_hi, xn_lo, wr_hi_ref, wr_lo_ref) + br_ref[...]
    lane = lax.broadcasted_iota(jnp.int32, logits.shape, 1)
    logits = jnp.where(lane < N_EXPERTS, logits, -jnp.inf)
    m1 = jnp.max(logits, axis=1, keepdims=True)
    i1 = jnp.min(jnp.where(logits == m1, lane, LANES), axis=1, keepdims=True)
    rest = jnp.where(lane == i1, -jnp.inf, logits)
    m2 = jnp.max(rest, axis=1, keepdims=True)
    i2 = jnp.min(jnp.where(rest == m2, lane, LANES), axis=1, keepdims=True)
    e2 = jnp.exp(m2 - m1)
    g1 = 1.0 / (1.0 + e2)
    g2 = e2 / (1.0 + e2)
    idx_ref[...] = jnp.where(lane == 0, i1, i2)
    gate_ref[...] = jnp.where(lane == 0, g1, g2)


def _router(x, g, wr, br, *, tm):
    m, d = x.shape
    wr_hi, wr_lo = _split_bf16(wr)
    return pl.pallas_call(
        _router_kernel,
        grid=(m // tm,),
        in_specs=[
            pl.BlockSpec((tm, d), lambda i: (i, 0)),
            pl.BlockSpec((1, d), lambda i: (0, 0)),
            pl.BlockSpec((d, LANES), lambda i: (0, 0)),
            pl.BlockSpec((d, LANES), lambda i: (0, 0)),
            pl.BlockSpec((1, LANES), lambda i: (0, 0)),
        ],
        out_specs=[
            pl.BlockSpec((tm, d), lambda i: (i, 0)),
            pl.BlockSpec((tm, LANES), lambda i: (i, 0)),
            pl.BlockSpec((tm, LANES), lambda i: (i, 0)),
        ],
        out_shape=[
            jax.ShapeDtypeStruct((m, d), F32),
            jax.ShapeDtypeStruct((m, LANES), jnp.int32),
            jax.ShapeDtypeStruct((m, LANES), F32),
        ],
        compiler_params=_params("parallel"),
        name="router",
    )(x, g, wr_hi, wr_lo, br)


def _row_copy(src_hbm, src_row, dst_ref, dst_row, sem):
    return pltpu.make_async_copy(src_hbm.at[pl.ds(src_row, 1)], dst_ref.at[pl.ds(dst_row, 1)], sem)


def _block_wait(src_hbm, dst_ref, sem):
    pltpu.make_async_copy(src_hbm.at[pl.ds(0, dst_ref.shape[0])], dst_ref, sem).wait()


MOE_GATHER_ROWS = 64
MOE_K_SPLIT = 2


def _moe_kernel(be_ref, nu_ref, tok_ref, x_hbm, w1_ref, w3_ref, w2_ref, out_ref,
                xg_ref, xb_ref, a_ref, b_ref, sem, *, tm):
    i = pl.program_id(0)
    j = pl.program_id(1)
    k = pl.program_id(2)
    step = j * MOE_K_SPLIT + k
    n_used = nu_ref[0]
    used = i < n_used
    gather_steps = tm // MOE_GATHER_ROWS
    gathering = jnp.logical_and(i + 1 < n_used, jnp.logical_and(step >= 1, step <= gather_steps))
    kt = xb_ref.shape[1] // MOE_K_SPLIT

    def gather(block, r0):
        for r in range(MOE_GATHER_ROWS):
            row = r0 + r
            _row_copy(x_hbm, tok_ref[block * tm + row], xg_ref, row, sem).start()

    @pl.when(jnp.logical_and(i == 0, step == 0))
    def _():
        def body(c, carry):
            gather(0, c * MOE_GATHER_ROWS)
            return carry
        lax.fori_loop(0, gather_steps, body, 0)

    @pl.when(jnp.logical_and(used, step == 0))
    def _():
        _block_wait(x_hbm, xg_ref, sem)
        xb_ref[...] = xg_ref[...].astype(BF16)

    @pl.when(step == 0)
    def _():
        out_ref[...] = jnp.zeros_like(out_ref)

    def compute(last_slab, with_gather):
        if with_gather:
            gather(i + 1, (step - 1) * MOE_GATHER_ROWS)
        xk = xb_ref[:, pl.ds(pl.multiple_of(k * kt, kt), kt)]
        pa = jnp.dot(xk, w1_ref[...].astype(BF16), preferred_element_type=F32)
        pb = jnp.dot(xk, w3_ref[...].astype(BF16), preferred_element_type=F32)
        if not last_slab:
            a_ref[...] = pa
            b_ref[...] = pb
        else:
            a = a_ref[...] + pa
            b = b_ref[...] + pb
            h = (a * _sigmoid(a) * b).astype(BF16)
            out_ref[...] += jnp.dot(h, w2_ref[...].astype(BF16), preferred_element_type=F32)

    for last_slab in (False, True):
        for with_gather in (True, False):
            cond = jnp.logical_and(jnp.logical_and(used, (k == MOE_K_SPLIT - 1) == last_slab),
                                   gathering == with_gather)
            pl.when(cond)(functools.partial(compute, last_slab, with_gather))


def _moe_experts(block_e, n_used, row_tok, x, w1, w3, w2, *, tm, tf):
    n_rows = row_tok.shape[0]
    d = x.shape[1]
    f = w1.shape[2]
    nf = f // tf
    kt = d // MOE_K_SPLIT
    assert MOE_K_SPLIT == 2 and tm % MOE_GATHER_ROWS == 0
    assert nf * MOE_K_SPLIT > tm // MOE_GATHER_ROWS

    def ff(i, j, nu):
        return jnp.where(i < nu[0], j, nf - 1)

    def kk(i, k, nu):
        return jnp.where(i < nu[0], k, MOE_K_SPLIT - 1)

    return pl.pallas_call(
        functools.partial(_moe_kernel, tm=tm),
        grid_spec=pltpu.PrefetchScalarGridSpec(
            num_scalar_prefetch=3,
            grid=(n_rows // tm, nf, MOE_K_SPLIT),
            in_specs=[
                pl.BlockSpec(memory_space=pl.ANY),
                pl.BlockSpec((None, kt, tf),
                             lambda i, j, k, be, nu, tok: (be[i], kk(i, k, nu), ff(i, j, nu))),
                pl.BlockSpec((None, kt, tf),
                             lambda i, j, k, be, nu, tok: (be[i], kk(i, k, nu), ff(i, j, nu))),
                pl.BlockSpec((None, tf, d), lambda i, j, k, be, nu, tok: (be[i], ff(i, j, nu), 0)),
            ],
            out_specs=pl.BlockSpec((tm, d), lambda i, j, k, be, nu, tok: (i, 0)),
            scratch_shapes=[pltpu.VMEM((tm, d), F32), pltpu.VMEM((tm, d), BF16),
                            pltpu.VMEM((tm, tf), F32), pltpu.VMEM((tm, tf), F32),
                            pltpu.SemaphoreType.DMA(())],
        ),
        out_shape=jax.ShapeDtypeStruct((n_rows, d), F32),
        compiler_params=_params("arbitrary", "arbitrary", "arbitrary"),
        name="moe_experts",
    )(block_e, n_used, row_tok, x, w1, w3, w2)


COMBINE_UNROLL = 8


def _combine_kernel(dest_ref, dest_next_ref, y_hbm, h_ref, gate_ref, fn_ref, out_ref,
                    buf_ref, sem, *, tc):
    i = pl.program_id(0)
    slot = i % 2

    def fetch(idx_ref, dst_slot):
        def body(r, carry):
            for k in range(TOP_K):
                _row_copy(y_hbm, idx_ref[TOP_K * r + k], buf_ref.at[dst_slot, k], r,
                          sem.at[dst_slot]).start()
            return carry
        lax.fori_loop(0, tc, body, 0, unroll=COMBINE_UNROLL)

    @pl.when(i == 0)
    def _():
        fetch(dest_ref, 0)

    @pl.when(i + 1 < pl.num_programs(0))
    def _():
        fetch(dest_next_ref, 1 - slot)

    for k in range(TOP_K):
        _block_wait(y_hbm, buf_ref.at[slot, k], sem.at[slot])
    gates = gate_ref[...]
    y = h_ref[...] + gates[:, 0:1] * buf_ref[slot, 0] + gates[:, 1:2] * buf_ref[slot, 1]
    out_ref[...] = _rms_norm(y, fn_ref[...])


def _combine(dest, y_rows, h, gates, fn, *, tc):
    m, d = h.shape
    last = m // tc - 1
    return pl.pallas_call(
        functools.partial(_combine_kernel, tc=tc),
        grid=(m // tc,),
        in_specs=[
            pl.BlockSpec((TOP_K * tc,), lambda i: (i,), memory_space=pltpu.SMEM),
            pl.BlockSpec((TOP_K * tc,), lambda i: (jnp.minimum(i + 1, last),),
                         memory_space=pltpu.SMEM),
            pl.BlockSpec(memory_space=pl.ANY),
            pl.BlockSpec((tc, d), lambda i: (i, 0)),
            pl.BlockSpec((tc, LANES), lambda i: (i, 0)),
            pl.BlockSpec((1, d), lambda i: (0, 0)),
        ],
        out_specs=pl.BlockSpec((tc, d), lambda i: (i, 0)),
        out_shape=jax.ShapeDtypeStruct((m, d), F32),
        scratch_shapes=[pltpu.VMEM((2, TOP_K, tc, d), F32), pltpu.SemaphoreType.DMA((2,))],
        compiler_params=_params("arbitrary"),
        name="moe_combine",
    )(dest, dest, y_rows, h, gates, fn)


def _dispatch_plan(top_idx, n_tok, tm):
    flat_e = top_idx.reshape(-1)
    n_asg = flat_e.shape[0]
    onehot = (flat_e[:, None] == jnp.arange(N_EXPERTS, dtype=jnp.int32)[None, :]).astype(jnp.int32)
    csum = jnp.cumsum(onehot, axis=0)
    rank = jnp.sum(onehot * csum, axis=1) - 1
    counts = csum[-1]
    padded = ((counts + tm - 1) // tm) * tm
    pad_end = jnp.cumsum(padded)
    pad_start = pad_end - padded
    dest = (pad_start[flat_e] + rank).astype(jnp.int32)
    n_blocks = n_asg // tm + N_EXPERTS
    n_used = (pad_end[-1] // tm).astype(jnp.int32)
    flat_tok = jnp.arange(n_asg, dtype=jnp.int32) // TOP_K
    row_tok = jnp.zeros((n_blocks * tm,), jnp.int32).at[dest].set(flat_tok)
    blk = jnp.minimum(jnp.arange(n_blocks, dtype=jnp.int32), n_used - 1) * tm
    block_e = jnp.minimum(jnp.searchsorted(pad_end, blk, side="right"), N_EXPERTS - 1).astype(jnp.int32)
    return dest, row_tok, block_e, n_used.reshape(1)


def _pad_lanes(a):
    return jnp.pad(a, [(0, 0)] * (a.ndim - 1) + [(0, LANES - a.shape[-1])])


def kernel(x, ev_norm_mix, ev_w_in, ev_conv_qk, ev_b_gates, ev_ml_norm, ev_w_out, ev_norm_ffn,
           ev_w1, ev_w3, ev_w2, od_norm_mix, od_pw1, od_dw, od_ln_g, od_ln_b, od_pw2,
           od_norm_ffn, od_router, od_router_b, od_moe_w1, od_moe_w3, od_moe_w2, final_norm):
    bsz, s, d = x.shape
    m = bsz * s
    h = x.reshape(m, d)

    w_in = ev_w_in[0]
    gate0 = 4 * ML_WIDTH
    gate1 = gate0 + 2 * ML_HEADS
    w_main = _column_tiles(jnp.concatenate([w_in[:, :gate0], w_in[:, gate1:]], axis=1), 1024)
    w_gates = _pad_lanes(w_in[:, gate0:gate1])
    b_gates = _pad_lanes(ev_b_gates[0][None, :])
    proj, gates = _in_proj(h, ev_norm_mix[0][None, :], w_main, w_gates, b_gates, tm=1024)
    proj = proj.reshape(bsz, s, -1)
    n_chunks = s // ML_CHUNK
    gates = gates[:, :2 * ML_HEADS].reshape(bsz, s, 2 * ML_HEADS).transpose(0, 2, 1)
    i_pre = gates[:, :ML_HEADS].reshape(bsz, ML_HEADS, n_chunks, ML_CHUNK)
    f_pre = gates[:, ML_HEADS:].reshape(bsz, ML_HEADS, n_chunks, ML_CHUNK)
    h_ml = _mlstm(proj, ev_conv_qk[0], i_pre, f_pre, ev_ml_norm[0][None, :], hps=2)
    h_sb = _stick_breaking(proj, gate0, tq=256, hps=8)
    h = _proj2_res(h, h_ml.reshape(m, ML_WIDTH), h_sb.reshape(m, SB_WIDTH),
                   ev_w_out[0].astype(BF16), tm=1024, tn=1024)

    h = _swiglu_res(h, ev_norm_ffn[0][None, :], _column_tiles(ev_w1[0], 512),
                    _column_tiles(ev_w3[0], 512), ev_w2[0].astype(BF16), tm=512)

    cw = od_pw1.shape[2] // 2
    u = _pw1_glu(h, od_norm_mix[0][None, :], _column_tiles(od_pw1[0], 512), tm=1024)
    u = _conv_ln(u.reshape(bsz, s, cw), od_dw[0], od_ln_g[0][None, :], od_ln_b[0][None, :], ts=256)
    h = _proj_res(h, u.reshape(m, cw), od_pw2[0].astype(BF16), tm=1024, tn=1024)

    tm_e = 1024
    xn, top_idx, top_gate = _router(h, od_norm_ffn[0][None, :], _pad_lanes(od_router[0]),
                                    _pad_lanes(od_router_b[0][None, :]), tm=512)
    dest, row_tok, block_e, n_used = _dispatch_plan(top_idx[:, :TOP_K], m, tm_e)
    y_rows = _moe_experts(block_e, n_used, row_tok, xn, od_moe_w1[0], od_moe_w3[0], od_moe_w2[0],
                          tm=tm_e, tf=512)
    out = _combine(dest, y_rows, h, top_gate, final_norm[None, :], tc=256)
    return out.reshape(bsz, s, d)
```

```python
import functools

import jax
import jax.numpy as jnp
from jax import lax
from jax.experimental import pallas as pl
from jax.experimental.pallas import tpu as pltpu

ML_HEADS = 4
ML_HEAD_DIM = 256
ML_WIDTH = ML_HEADS * ML_HEAD_DIM
SB_HEADS = 8
SB_HEAD_DIM = 128
SB_WIDTH = SB_HEADS * SB_HEAD_DIM
QK_CONV = 4
ML_CHUNK = 128
CONV_WIDTH = 31
N_EXPERTS = 8
TOP_K = 2
EPS = 1e-6

BF16 = jnp.bfloat16
F32 = jnp.float32

LANES = 128
F32_SUBLANES = 8
BF16_SUBLANES = 16
V7X_VMEM_LIMIT = 56 * 1024 * 1024

NT_DIMS = (((1,), (1,)), ((), ()))
TN_DIMS = (((0,), (0,)), ((), ()))


def _params(*semantics):
    return pltpu.CompilerParams(dimension_semantics=semantics, vmem_limit_bytes=V7X_VMEM_LIMIT)


def _rms_norm(x, g):
    return x * lax.rsqrt(jnp.mean(x * x, axis=-1, keepdims=True) + EPS) * g


def _sigmoid(x):
    return 1.0 / (1.0 + jnp.exp(-x))


def _softplus(z):
    return jnp.maximum(z, 0.0) + jnp.log(1.0 + jnp.exp(-jnp.abs(z)))


def _split_bf16(a):
    hi = a.astype(BF16)
    lo = (a - hi.astype(F32)).astype(BF16)
    return hi, lo


def _dot_3pass(x_hi, x_lo, w_hi_ref, w_lo_ref):
    w_hi = w_hi_ref[...]
    return (jnp.dot(x_hi, w_hi, preferred_element_type=F32)
            + jnp.dot(x_lo, w_hi, preferred_element_type=F32)
            + jnp.dot(x_hi, w_lo_ref[...], preferred_element_type=F32))


def _in_proj_kernel(x_ref, g_ref, w_ref, wg_hi_ref, wg_lo_ref, bg_ref, proj_ref, gates_ref, xn_ref):
    @pl.when(pl.program_id(1) == 0)
    def _():
        xn_hi, xn_lo = _split_bf16(_rms_norm(x_ref[...], g_ref[...]))
        xn_ref[...] = xn_hi
        gates_ref[...] = _dot_3pass(xn_hi, xn_lo, wg_hi_ref, wg_lo_ref) + bg_ref[...]

    proj_ref[...] = jnp.dot(xn_ref[...], w_ref[...], preferred_element_type=F32).astype(BF16)


def _in_proj(x, g, w, wg, bg, *, tm, tn):
    m, d = x.shape
    n = w.shape[1]
    wg_hi, wg_lo = _split_bf16(wg)
    return pl.pallas_call(
        _in_proj_kernel,
        grid=(m // tm, n // tn),
        in_specs=[
            pl.BlockSpec((tm, d), lambda i, j: (i, 0)),
            pl.BlockSpec((1, d), lambda i, j: (0, 0)),
            pl.BlockSpec((d, tn), lambda i, j: (0, j)),
            pl.BlockSpec((d, LANES), lambda i, j: (0, 0)),
            pl.BlockSpec((d, LANES), lambda i, j: (0, 0)),
            pl.BlockSpec((1, LANES), lambda i, j: (0, 0)),
        ],
        out_specs=[
            pl.BlockSpec((tm, tn), lambda i, j: (i, j)),
            pl.BlockSpec((tm, LANES), lambda i, j: (i, 0)),
        ],
        out_shape=[jax.ShapeDtypeStruct((m, n), BF16), jax.ShapeDtypeStruct((m, LANES), F32)],
        scratch_shapes=[pltpu.VMEM((tm, d), BF16)],
        compiler_params=_params("parallel", "arbitrary"),
        name="in_proj",
    )(x, g, w, wg_hi, wg_lo, bg)


def _mlstm_kernel(q_ref, k_ref, v_ref, o_ref, cq_ref, ck_ref, i_ref, f_ref, nrm_ref,
                  out_ref, c_state, n_state, m_state, *, hps):
    L = ML_CHUNK
    d = ML_HEAD_DIM
    n_chunks = q_ref.shape[0] // L
    halo = BF16_SUBLANES

    c_state[...] = jnp.zeros_like(c_state)
    n_state[...] = jnp.zeros_like(n_state)
    m_state[...] = jnp.zeros_like(m_state)

    row = lax.broadcasted_iota(jnp.int32, (L, L), 0)
    col = lax.broadcasted_iota(jnp.int32, (L, L), 1)
    causal = col <= row
    eye = col == row

    def conv_swish(x_ref, w_ref, c, lanes):
        start = pl.multiple_of(c * L, L)
        cur = x_ref[pl.ds(start, L), lanes].astype(F32)
        prev_start = pl.multiple_of(jnp.maximum(start - halo, 0), halo)
        prev = x_ref[pl.ds(prev_start, halo), lanes].astype(F32)
        prev = jnp.where(c > 0, prev, 0.0)
        xx = jnp.concatenate([prev, cur], axis=0)
        w = w_ref[:, lanes]
        y = jnp.zeros((L, d), F32)
        for j in range(QK_CONV):
            off = halo - (QK_CONV - 1) + j
            r = off % F32_SUBLANES
            shifted = xx if r == 0 else pltpu.roll(xx, halo + L - r, axis=0)
            y = y + w[j:j + 1, :] * shifted[off - r:off - r + L, :]
        return y * _sigmoid(y)

    def to_col(r):
        return jnp.sum(jnp.where(eye, r, 0.0), axis=1, keepdims=True)

    def head_chunk(hh, c):
        lanes = slice(hh * d, (hh + 1) * d)
        start = pl.multiple_of(c * L, L)
        q = conv_swish(q_ref, cq_ref, c, lanes)
        k = conv_swish(k_ref, ck_ref, c, lanes) * (d ** -0.5)
        v = v_ref[pl.ds(start, L), lanes]
        li_row = i_ref[hh, pl.ds(c, 1), :]
        f_row = f_ref[hh, pl.ds(c, 1), :]
        lf_row = -_softplus(-f_row)
        lf_col = to_col(lf_row)
        li_col = to_col(li_row)
        bl_col = jnp.sum(jnp.where(causal, lf_row, 0.0), axis=1, keepdims=True)
        bl_row = jnp.sum(jnp.where(col >= row, lf_col, 0.0), axis=0, keepdims=True)
        g = jnp.sum(lf_row, axis=1, keepdims=True)

        c_prev = c_state[hh]
        n_prev = n_state[hh]
        m_prev = m_state[hh]

        d_log = jnp.where(causal, bl_col - bl_row + li_row, -jnp.inf)
        inter = bl_col + m_prev
        m_t = jnp.maximum(inter, jnp.max(d_log, axis=1, keepdims=True))
        w_intra = jnp.exp(d_log - m_t)
        w_inter = jnp.exp(inter - m_t)
        q_b = q.astype(BF16)
        k_b = k.astype(BF16)
        s_qk = lax.dot_general(q_b, k_b, NT_DIMS, preferred_element_type=F32) * w_intra
        num = (w_inter * jnp.dot(q_b, c_prev.astype(BF16), preferred_element_type=F32)
               + jnp.dot(s_qk.astype(BF16), v, preferred_element_type=F32))
        qn = (w_inter * jnp.sum(q * n_prev, axis=1, keepdims=True)
              + jnp.sum(s_qk, axis=1, keepdims=True))
        denom = jnp.maximum(jnp.abs(qn), jnp.exp(-m_t))
        h = num / denom
        h = h * lax.rsqrt(jnp.mean(h * h, axis=1, keepdims=True) + EPS)
        h = h * nrm_ref[:, lanes] * _sigmoid(o_ref[pl.ds(start, L), lanes].astype(F32))
        out_ref[pl.ds(start, L), lanes] = h.astype(out_ref.dtype)

        a_col = g - bl_col + li_col
        m_loc = jnp.max(a_col, axis=0, keepdims=True)
        kw = k * jnp.exp(a_col - m_loc)
        c_loc = lax.dot_general(kw.astype(BF16), v, TN_DIMS, preferred_element_type=F32)
        n_loc = jnp.sum(kw, axis=0, keepdims=True)
        m_new = jnp.maximum(g + m_prev, m_loc)
        s_prev = jnp.exp(g + m_prev - m_new)
        s_loc = jnp.exp(m_loc - m_new)
        c_state[hh] = s_prev * c_prev + s_loc * c_loc
        n_state[hh] = s_prev * n_prev + s_loc * n_loc
        m_state[hh] = m_new

    def chunk(c, carry):
        for hh in range(hps):
            head_chunk(hh, c)
        return carry

    lax.fori_loop(0, n_chunks, chunk, 0)


def _mlstm(proj, conv_qk, i_pre, f_pre, ml_norm, *, hps):
    b, s, _ = proj.shape
    d = ML_HEAD_DIM
    w = hps * d
    groups = ML_HEADS // hps
    n_chunks = s // ML_CHUNK

    def col_spec(part):
        return pl.BlockSpec((None, s, w), lambda bi, hi: (bi, 0, part * groups + hi))

    gate_spec = pl.BlockSpec((None, hps, n_chunks, ML_CHUNK), lambda bi, hi: (bi, hi, 0, 0))
    return pl.pallas_call(
        functools.partial(_mlstm_kernel, hps=hps),
        grid=(b, groups),
        in_specs=[
            col_spec(0), col_spec(1), col_spec(2), col_spec(3),
            pl.BlockSpec((QK_CONV, w), lambda bi, hi: (0, hi)),
            pl.BlockSpec((QK_CONV, w), lambda bi, hi: (0, groups + hi)),
            gate_spec, gate_spec,
            pl.BlockSpec((1, w), lambda bi, hi: (0, hi)),
        ],
        out_specs=pl.BlockSpec((None, s, w), lambda bi, hi: (bi, 0, hi)),
        out_shape=jax.ShapeDtypeStruct((b, s, ML_WIDTH), BF16),
        scratch_shapes=[pltpu.VMEM((hps, d, d), F32), pltpu.VMEM((hps, 1, d), F32),
                        pltpu.VMEM((hps, 1, 1), F32)],
        compiler_params=_params("parallel", "parallel"),
        name="mlstm",
    )(proj, proj, proj, proj, conv_qk, conv_qk, i_pre, f_pre, ml_norm)


SB_DEAD_LOG = 104.0


def _sb_kernel(q_ref, k_ref, v_ref, out_ref, *, tq, hps):
    qi = pl.program_id(2)
    d = SB_HEAD_DIM
    scale = d ** -0.5
    row = lax.broadcasted_iota(jnp.int32, (tq, tq), 0)
    col = lax.broadcasted_iota(jnp.int32, (tq, tq), 1)
    strict = col < row
    suffix = (row > col).astype(BF16)

    def block(kb, carry, diag):
        start = pl.multiple_of(kb * tq, tq)
        heads = range(hps)
        lanes = [slice(hh * d, (hh + 1) * d) for hh in heads]
        z = [lax.dot_general(q_ref[:, lanes[hh]], k_ref[pl.ds(start, tq), lanes[hh]], NT_DIMS,
                             preferred_element_type=F32) * scale for hh in heads]
        sp, log_beta, sp_hi, sp_lo = [], [], [], []
        for hh in heads:
            s = _softplus(z[hh])
            log_beta.append(z[hh] - s)
            if diag:
                s = jnp.where(strict, s, 0.0)
            hi = s.astype(BF16)
            sp.append(s)
            sp_hi.append(hi)
            sp_lo.append((s - hi.astype(F32)).astype(BF16))
        rest = [jnp.dot(sp_hi[hh], suffix, preferred_element_type=F32)
                + jnp.dot(sp_lo[hh], suffix, preferred_element_type=F32) for hh in heads]
        w = []
        for hh in heads:
            wh = jnp.exp(log_beta[hh] - rest[hh] - carry[hh][0])
            if diag:
                wh = jnp.where(strict, wh, 0.0)
            w.append(wh.astype(BF16))
        out = []
        for hh in heads:
            acc = carry[hh][1] + jnp.dot(w[hh], v_ref[pl.ds(start, tq), lanes[hh]],
                                         preferred_element_type=F32)
            run = carry[hh][0] + rest[hh][:, 0:1] + sp[hh][:, 0:1]
            out.append((run, acc))
        return tuple(out)

    def live(carry):
        low = carry[0][0]
        for hh in range(1, hps):
            low = jnp.minimum(low, carry[hh][0])
        return jnp.min(low) < SB_DEAD_LOG

    init = tuple((jnp.zeros((tq, 1), F32), jnp.zeros((tq, d), F32)) for _ in range(hps))
    carry = block(qi, init, True)

    def step(state):
        t, c, _ = state
        c = block(qi - 1 - t, c, False)
        return t + 1, c, live(c)

    _, carry, _ = lax.while_loop(lambda s: jnp.logical_and(s[0] < qi, s[2]), step,
                                 (jnp.int32(0), carry, live(carry)))
    for hh in range(hps):
        out_ref[:, hh * d:(hh + 1) * d] = carry[hh][1].astype(out_ref.dtype)


def _stick_breaking(proj, col0, *, tq, hps):
    b, s, _ = proj.shape
    w = hps * SB_HEAD_DIM
    groups = SB_HEADS // hps
    c0 = col0 // w
    return pl.pallas_call(
        functools.partial(_sb_kernel, tq=tq, hps=hps),
        grid=(b, groups, s // tq),
        in_specs=[
            pl.BlockSpec((None, tq, w), lambda bi, hi, qi: (bi, qi, c0 + hi)),
            pl.BlockSpec((None, s, w), lambda bi, hi, qi: (bi, 0, c0 + groups + hi)),
            pl.BlockSpec((None, s, w), lambda bi, hi, qi: (bi, 0, c0 + 2 * groups + hi)),
        ],
        out_specs=pl.BlockSpec((None, tq, w), lambda bi, hi, qi: (bi, qi, hi)),
        out_shape=jax.ShapeDtypeStruct((b, s, SB_WIDTH), BF16),
        compiler_params=_params("parallel", "parallel", "arbitrary"),
        name="stick_breaking",
    )(proj, proj, proj)


def _proj2_res_kernel(x_ref, a_ref, b_ref, wa_ref, wb_ref, out_ref):
    out_ref[...] = (x_ref[...]
                    + jnp.dot(a_ref[...], wa_ref[...], preferred_element_type=F32)
                    + jnp.dot(b_ref[...], wb_ref[...], preferred_element_type=F32))


def _proj2_res(x, a, b, w, *, tm, tn):
    m, n = x.shape
    ka, kb = a.shape[1], b.shape[1]
    assert ka == kb and w.shape[0] == ka + kb
    return pl.pallas_call(
        _proj2_res_kernel,
        grid=(m // tm, n // tn),
        in_specs=[
            pl.BlockSpec((tm, tn), lambda i, j: (i, j)),
            pl.BlockSpec((tm, ka), lambda i, j: (i, 0)),
            pl.BlockSpec((tm, kb), lambda i, j: (i, 0)),
            pl.BlockSpec((ka, tn), lambda i, j: (0, j)),
            pl.BlockSpec((kb, tn), lambda i, j: (1, j)),
        ],
        out_specs=pl.BlockSpec((tm, tn), lambda i, j: (i, j)),
        out_shape=jax.ShapeDtypeStruct((m, n), F32),
        compiler_params=_params("parallel", "arbitrary"),
        name="out_proj",
    )(x, a, b, w, w)


def _proj_res_kernel(x_ref, a_ref, w_ref, out_ref):
    out_ref[...] = x_ref[...] + jnp.dot(a_ref[...], w_ref[...], preferred_element_type=F32)


def _proj_res(x, a, w, *, tm, tn):
    m, n = x.shape
    k = a.shape[1]
    return pl.pallas_call(
        _proj_res_kernel,
        grid=(m // tm, n // tn),
        in_specs=[
            pl.BlockSpec((tm, tn), lambda i, j: (i, j)),
            pl.BlockSpec((tm, k), lambda i, j: (i, 0)),
            pl.BlockSpec((k, tn), lambda i, j: (0, j)),
        ],
        out_specs=pl.BlockSpec((tm, tn), lambda i, j: (i, j)),
        out_shape=jax.ShapeDtypeStruct((m, n), F32),
        compiler_params=_params("parallel", "arbitrary"),
        name="pw2",
    )(x, a, w)


def _swiglu_kernel(x_ref, g_ref, w1_ref, w3_ref, w2_ref, out_ref, xn_ref, acc_ref):
    j = pl.program_id(1)

    @pl.when(j == 0)
    def _():
        xn_ref[...] = _rms_norm(x_ref[...], g_ref[...]).astype(BF16)
        acc_ref[...] = jnp.zeros_like(acc_ref)

    xn = xn_ref[...]
    a = jnp.dot(xn, w1_ref[...], preferred_element_type=F32)
    b = jnp.dot(xn, w3_ref[...], preferred_element_type=F32)
    h = (a * _sigmoid(a) * b).astype(BF16)
    acc_ref[...] += jnp.dot(h, w2_ref[...], preferred_element_type=F32)

    @pl.when(j == pl.num_programs(1) - 1)
    def _():
        out_ref[...] = x_ref[...] + acc_ref[...]


def _swiglu_res(x, g, w1, w3, w2, *, tm, tf):
    m, d = x.shape
    f = w1.shape[1]
    return pl.pallas_call(
        _swiglu_kernel,
        grid=(m // tm, f // tf),
        in_specs=[
            pl.BlockSpec((tm, d), lambda i, j: (i, 0)),
            pl.BlockSpec((1, d), lambda i, j: (0, 0)),
            pl.BlockSpec((d, tf), lambda i, j: (0, j)),
            pl.BlockSpec((d, tf), lambda i, j: (0, j)),
            pl.BlockSpec((tf, d), lambda i, j: (j, 0)),
        ],
        out_specs=pl.BlockSpec((tm, d), lambda i, j: (i, 0)),
        out_shape=jax.ShapeDtypeStruct((m, d), F32),
        scratch_shapes=[pltpu.VMEM((tm, d), BF16), pltpu.VMEM((tm, d), F32)],
        compiler_params=_params("parallel", "arbitrary"),
        name="swiglu",
    )(x, g, w1, w3, w2)


def _pw1_glu_kernel(x_ref, g_ref, wa_ref, wg_ref, out_ref, xn_ref):
    @pl.when(pl.program_id(1) == 0)
    def _():
        xn_ref[...] = _rms_norm(x_ref[...], g_ref[...]).astype(BF16)

    xn = xn_ref[...]
    a = jnp.dot(xn, wa_ref[...], preferred_element_type=F32)
    gate = jnp.dot(xn, wg_ref[...], preferred_element_type=F32)
    out_ref[...] = (a * _sigmoid(gate)).astype(out_ref.dtype)


def _pw1_glu(x, g, w, *, tm, tn):
    m, d = x.shape
    n = w.shape[1] // 2
    gate0 = n // tn
    return pl.pallas_call(
        _pw1_glu_kernel,
        grid=(m // tm, n // tn),
        in_specs=[
            pl.BlockSpec((tm, d), lambda i, j: (i, 0)),
            pl.BlockSpec((1, d), lambda i, j: (0, 0)),
            pl.BlockSpec((d, tn), lambda i, j: (0, j)),
            pl.BlockSpec((d, tn), lambda i, j: (0, gate0 + j)),
        ],
        out_specs=pl.BlockSpec((tm, tn), lambda i, j: (i, j)),
        out_shape=jax.ShapeDtypeStruct((m, n), BF16),
        scratch_shapes=[pltpu.VMEM((tm, d), BF16)],
        compiler_params=_params("parallel", "arbitrary"),
        name="pw1_glu",
    )(x, g, w, w)


CONV_HALO = 32
CONV_ROWS = 64


def _conv_ln_kernel(halo_ref, u_ref, w_ref, g_ref, b_ref, out_ref, xx_ref, y_ref, *, ts):
    ch = u_ref.shape[1]
    first = pl.program_id(1) == 0
    halo = halo_ref[...].astype(F32)
    xx_ref[pl.ds(0, CONV_HALO), :] = jnp.where(first, 0.0, halo)
    xx_ref[pl.ds(CONV_HALO, ts), :] = u_ref[...].astype(F32)

    span = CONV_ROWS + CONV_HALO
    first_tap = CONV_HALO - (CONV_WIDTH - 1)

    def lane_block(cb, carry):
        c0 = pl.multiple_of(cb * LANES, LANES)
        w = w_ref[:, pl.ds(c0, LANES)]
        for rb in range(ts // CONV_ROWS):
            window = xx_ref[pl.ds(rb * CONV_ROWS, span), pl.ds(c0, LANES)]
            acc = jnp.zeros((CONV_ROWS, LANES), F32)
            for r in range(F32_SUBLANES):
                shifted = window if r == 0 else pltpu.roll(window, span - r, axis=0)
                for off in range(r, CONV_HALO + 1, F32_SUBLANES):
                    j = off - first_tap
                    if 0 <= j < CONV_WIDTH:
                        acc = acc + w[j:j + 1, :] * shifted[off - r:off - r + CONV_ROWS, :]
            y_ref[pl.ds(rb * CONV_ROWS, CONV_ROWS), pl.ds(c0, LANES)] = acc
        return carry

    lax.fori_loop(0, ch // LANES, lane_block, 0)

    y = y_ref[...]
    mu = jnp.mean(y, axis=-1, keepdims=True)
    yc = y - mu
    yn = yc * lax.rsqrt(jnp.mean(yc * yc, axis=-1, keepdims=True) + EPS) * g_ref[...] + b_ref[...]
    out_ref[...] = (yn * _sigmoid(yn)).astype(out_ref.dtype)


def _conv_ln(u, w, g, b, *, ts):
    bsz, s, ch = u.shape
    per = ts // CONV_HALO
    return pl.pallas_call(
        functools.partial(_conv_ln_kernel, ts=ts),
        grid=(bsz, s // ts),
        in_specs=[
            pl.BlockSpec((None, CONV_HALO, ch), lambda bi, i: (bi, jnp.maximum(i * per - 1, 0), 0)),
            pl.BlockSpec((None, ts, ch), lambda bi, i: (bi, i, 0)),
            pl.BlockSpec((CONV_WIDTH, ch), lambda bi, i: (0, 0)),
            pl.BlockSpec((1, ch), lambda bi, i: (0, 0)),
            pl.BlockSpec((1, ch), lambda bi, i: (0, 0)),
        ],
        out_specs=pl.BlockSpec((None, ts, ch), lambda bi, i: (bi, i, 0)),
        out_shape=jax.ShapeDtypeStruct((bsz, s, ch), BF16),
        scratch_shapes=[pltpu.VMEM((CONV_HALO + ts, ch), F32), pltpu.VMEM((ts, ch), F32)],
        compiler_params=_params("parallel", "parallel"),
        name="conv_ln",
    )(u, u, w, g, b)


def _router_kernel(x_ref, g_ref, wr_hi_ref, wr_lo_ref, br_ref, xn_ref, idx_ref, gate_ref):
    xn = _rms_norm(x_ref[...], g_ref[...])
    xn_ref[...] = xn
    xn_hi, xn_lo = _split_bf16(xn)
    logits = _dot_3pass(xn_hi, xn_lo, wr_hi_ref, wr_lo_ref) + br_ref[...]
    lane = lax.broadcasted_iota(jnp.int32, logits.shape, 1)
    logits = jnp.where(lane < N_EXPERTS, logits, -jnp.inf)
    m1 = jnp.max(logits, axis=1, keepdims=True)
    i1 = jnp.min(jnp.where(logits == m1, lane, LANES), axis=1, keepdims=True)
    rest = jnp.where(lane == i1, -jnp.inf, logits)
    m2 = jnp.max(rest, axis=1, keepdims=True)
    i2 = jnp.min(jnp.where(rest == m2, lane, LANES), axis=1, keepdims=True)
    e2 = jnp.exp(m2 - m1)
    g1 = 1.0 / (1.0 + e2)
    g2 = e2 / (1.0 + e2)
    idx_ref[...] = jnp.where(lane == 0, i1, i2)
    gate_ref[...] = jnp.where(lane == 0, g1, g2)


def _router(x, g, wr, br, *, tm):
    m, d = x.shape
    wr_hi, wr_lo = _split_bf16(wr)
    return pl.pallas_call(
        _router_kernel,
        grid=(m // tm,),
        in_specs=[
            pl.BlockSpec((tm, d), lambda i: (i, 0)),
            pl.BlockSpec((1, d), lambda i: (0, 0)),
            pl.BlockSpec((d, LANES), lambda i: (0, 0)),
            pl.BlockSpec((d, LANES), lambda i: (0, 0)),
            pl.BlockSpec((1, LANES), lambda i: (0, 0)),
        ],
        out_specs=[
            pl.BlockSpec((tm, d), lambda i: (i, 0)),
            pl.BlockSpec((tm, LANES), lambda i: (i, 0)),
            pl.BlockSpec((tm, LANES), lambda i: (i, 0)),
        ],
        out_shape=[
            jax.ShapeDtypeStruct((m, d), F32),
            jax.ShapeDtypeStruct((m, LANES), jnp.int32),
            jax.ShapeDtypeStruct((m, LANES), F32),
        ],
        compiler_params=_params("parallel"),
        name="router",
    )(x, g, wr_hi, wr_lo, br)


def _row_copy(src_hbm, src_row, dst_ref, dst_row, sem):
    return pltpu.make_async_copy(src_hbm.at[pl.ds(src_row, 1)], dst_ref.at[pl.ds(dst_row, 1)], sem)


def _block_wait(src_hbm, dst_ref, sem):
    pltpu.make_async_copy(src_hbm.at[pl.ds(0, dst_ref.shape[0])], dst_ref, sem).wait()


MOE_GATHER_ROWS = 64
MOE_BLOCK_FRACTIONS = (4, 2, 1)


def _moe_kernel(be_ref, nu_ref, rows_ref, tok_ref, x_hbm, w1_ref, w3_ref, w2_ref, out_ref,
                xg_ref, xb_ref, sem, *, tm):
    i = pl.program_id(0)
    j = pl.program_id(1)
    n_used = nu_ref[0]
    used = i < n_used
    rows = rows_ref[i]
    gather_steps = tm // MOE_GATHER_ROWS
    gathering = jnp.logical_and(i + 1 < n_used, jnp.logical_and(j >= 1, j <= gather_steps))

    def gather(block, r0):
        for r in range(MOE_GATHER_ROWS):
            row = r0 + r
            _row_copy(x_hbm, tok_ref[block * tm + row], xg_ref, row, sem).start()

    @pl.when(jnp.logical_and(i == 0, j == 0))
    def _():
        def body(c, carry):
            gather(0, c * MOE_GATHER_ROWS)
            return carry
        lax.fori_loop(0, gather_steps, body, 0)

    @pl.when(jnp.logical_and(used, j == 0))
    def _():
        _block_wait(x_hbm, xg_ref, sem)
        xb_ref[...] = xg_ref[...].astype(BF16)

    @pl.when(j == 0)
    def _():
        out_ref[...] = jnp.zeros_like(out_ref)

    def compute(m, with_gather):
        if with_gather:
            gather(i + 1, (j - 1) * MOE_GATHER_ROWS)
        xb = xb_ref[pl.ds(0, m), :]
        a = jnp.dot(xb, w1_ref[...].astype(BF16), preferred_element_type=F32)
        b = jnp.dot(xb, w3_ref[...].astype(BF16), preferred_element_type=F32)
        h = (a * _sigmoid(a) * b).astype(BF16)
        out_ref[pl.ds(0, m), :] += jnp.dot(h, w2_ref[...].astype(BF16), preferred_element_type=F32)

    lower = 0
    for m in MOE_BLOCK_FRACTIONS:
        rows_m = tm // m
        fits = jnp.logical_and(rows > lower, rows <= rows_m)
        lower = rows_m
        for with_gather in (True, False):
            cond = jnp.logical_and(fits, gathering == with_gather)
            pl.when(cond)(functools.partial(compute, rows_m, with_gather))


def _moe_experts(block_e, n_used, block_rows, row_tok, x, w1, w3, w2, *, tm, tf):
    n_rows = row_tok.shape[0]
    d = x.shape[1]
    f = w1.shape[2]
    nf = f // tf
    assert tm % MOE_GATHER_ROWS == 0 and nf > tm // MOE_GATHER_ROWS

    def ff(i, j, nu):
        return jnp.where(i < nu[0], j, nf - 1)

    return pl.pallas_call(
        functools.partial(_moe_kernel, tm=tm),
        grid_spec=pltpu.PrefetchScalarGridSpec(
            num_scalar_prefetch=4,
            grid=(n_rows // tm, nf),
            in_specs=[
                pl.BlockSpec(memory_space=pl.ANY),
                pl.BlockSpec((None, d, tf), lambda i, j, be, nu, rows, tok: (be[i], 0, ff(i, j, nu))),
                pl.BlockSpec((None, d, tf), lambda i, j, be, nu, rows, tok: (be[i], 0, ff(i, j, nu))),
                pl.BlockSpec((None, tf, d), lambda i, j, be, nu, rows, tok: (be[i], ff(i, j, nu), 0)),
            ],
            out_specs=pl.BlockSpec((tm, d), lambda i, j, be, nu, rows, tok: (i, 0)),
            scratch_shapes=[pltpu.VMEM((tm, d), F32), pltpu.VMEM((tm, d), BF16),
                            pltpu.SemaphoreType.DMA(())],
        ),
        out_shape=jax.ShapeDtypeStruct((n_rows, d), F32),
        compiler_params=_params("arbitrary", "arbitrary"),
        name="moe_experts",
    )(block_e, n_used, block_rows, row_tok, x, w1, w3, w2)


COMBINE_UNROLL = 8


def _combine_kernel(dest_ref, dest_next_ref, y_hbm, h_ref, gate_ref, fn_ref, out_ref,
                    buf_ref, sem, *, tc):
    i = pl.program_id(0)
    slot = i % 2

    def fetch(idx_ref, dst_slot):
        def body(r, carry):
            for k in range(TOP_K):
                _row_copy(y_hbm, idx_ref[TOP_K * r + k], buf_ref.at[dst_slot, k], r,
                          sem.at[dst_slot]).start(priority=k % 2)
            return carry
        lax.fori_loop(0, tc, body, 0, unroll=COMBINE_UNROLL)

    @pl.when(i == 0)
    def _():
        fetch(dest_ref, 0)

    @pl.when(i + 1 < pl.num_programs(0))
    def _():
        fetch(dest_next_ref, 1 - slot)

    for k in range(TOP_K):
        _block_wait(y_hbm, buf_ref.at[slot, k], sem.at[slot])
    gates = gate_ref[...]
    y = h_ref[...] + gates[:, 0:1] * buf_ref[slot, 0] + gates[:, 1:2] * buf_ref[slot, 1]
    out_ref[...] = _rms_norm(y, fn_ref[...])


def _combine(dest, y_rows, h, gates, fn, *, tc):
    m, d = h.shape
    last = m // tc - 1
    return pl.pallas_call(
        functools.partial(_combine_kernel, tc=tc),
        grid=(m // tc,),
        in_specs=[
            pl.BlockSpec((TOP_K * tc,), lambda i: (i,), memory_space=pltpu.SMEM),
            pl.BlockSpec((TOP_K * tc,), lambda i: (jnp.minimum(i + 1, last),),
                         memory_space=pltpu.SMEM),
            pl.BlockSpec(memory_space=pl.ANY),
            pl.BlockSpec((tc, d), lambda i: (i, 0)),
            pl.BlockSpec((tc, LANES), lambda i: (i, 0)),
            pl.BlockSpec((1, d), lambda i: (0, 0)),
        ],
        out_specs=pl.BlockSpec((tc, d), lambda i: (i, 0)),
        out_shape=jax.ShapeDtypeStruct((m, d), F32),
        scratch_shapes=[pltpu.VMEM((2, TOP_K, tc, d), F32), pltpu.SemaphoreType.DMA((2,))],
        compiler_params=_params("arbitrary"),
        name="moe_combine",
    )(dest, dest, y_rows, h, gates, fn)


def _dispatch_plan(top_idx, n_tok, tm):
    flat_e = top_idx.reshape(-1)
    n_asg = flat_e.shape[0]
    onehot = (flat_e[:, None] == jnp.arange(N_EXPERTS, dtype=jnp.int32)[None, :]).astype(jnp.int32)
    csum = jnp.cumsum(onehot, axis=0)
    rank = jnp.sum(onehot * csum, axis=1) - 1
    counts = csum[-1]
    padded = ((counts + tm - 1) // tm) * tm
    pad_end = jnp.cumsum(padded)
    pad_start = pad_end - padded
    dest = (pad_start[flat_e] + rank).astype(jnp.int32)
    n_blocks = n_asg // tm + N_EXPERTS
    n_used = (pad_end[-1] // tm).astype(jnp.int32)
    flat_tok = jnp.arange(n_asg, dtype=jnp.int32) // TOP_K
    row_tok = jnp.zeros((n_blocks * tm,), jnp.int32).at[dest].set(flat_tok)
    blk = jnp.minimum(jnp.arange(n_blocks, dtype=jnp.int32), n_used - 1) * tm
    block_e = jnp.minimum(jnp.searchsorted(pad_end, blk, side="right"), N_EXPERTS - 1).astype(jnp.int32)
    first = jnp.arange(n_blocks, dtype=jnp.int32) * tm
    block_rows = jnp.clip(pad_start[block_e] + counts[block_e] - first, 0, tm).astype(jnp.int32)
    return dest, row_tok, block_e, block_rows, n_used.reshape(1)


def _pad_lanes(a):
    return jnp.pad(a, [(0, 0)] * (a.ndim - 1) + [(0, LANES - a.shape[-1])])


def kernel(x, ev_norm_mix, ev_w_in, ev_conv_qk, ev_b_gates, ev_ml_norm, ev_w_out, ev_norm_ffn,
           ev_w1, ev_w3, ev_w2, od_norm_mix, od_pw1, od_dw, od_ln_g, od_ln_b, od_pw2,
           od_norm_ffn, od_router, od_router_b, od_moe_w1, od_moe_w3, od_moe_w2, final_norm):
    bsz, s, d = x.shape
    m = bsz * s
    h = x.reshape(m, d)

    w_in = ev_w_in[0]
    gate0 = 4 * ML_WIDTH
    gate1 = gate0 + 2 * ML_HEADS
    w_main = jnp.concatenate([w_in[:, :gate0], w_in[:, gate1:]], axis=1).astype(BF16)
    w_gates = _pad_lanes(w_in[:, gate0:gate1])
    b_gates = _pad_lanes(ev_b_gates[0][None, :])
    proj, gates = _in_proj(h, ev_norm_mix[0][None, :], w_main, w_gates, b_gates, tm=1024, tn=1024)
    proj = proj.reshape(bsz, s, -1)
    n_chunks = s // ML_CHUNK
    gates = gates[:, :2 * ML_HEADS].reshape(bsz, s, 2 * ML_HEADS).transpose(0, 2, 1)
    i_pre = gates[:, :ML_HEADS].reshape(bsz, ML_HEADS, n_chunks, ML_CHUNK)
    f_pre = gates[:, ML_HEADS:].reshape(bsz, ML_HEADS, n_chunks, ML_CHUNK)
    h_ml = _mlstm(proj, ev_conv_qk[0], i_pre, f_pre, ev_ml_norm[0][None, :], hps=2)
    h_sb = _stick_breaking(proj, gate0, tq=256, hps=8)
    h = _proj2_res(h, h_ml.reshape(m, ML_WIDTH), h_sb.reshape(m, SB_WIDTH),
                   ev_w_out[0].astype(BF16), tm=1024, tn=1024)

    h = _swiglu_res(h, ev_norm_ffn[0][None, :], ev_w1[0].astype(BF16), ev_w3[0].astype(BF16),
                    ev_w2[0].astype(BF16), tm=512, tf=512)

    cw = od_pw1.shape[2] // 2
    u = _pw1_glu(h, od_norm_mix[0][None, :], od_pw1[0].astype(BF16), tm=1024, tn=512)
    u = _conv_ln(u.reshape(bsz, s, cw), od_dw[0], od_ln_g[0][None, :], od_ln_b[0][None, :], ts=256)
    h = _proj_res(h, u.reshape(m, cw), od_pw2[0].astype(BF16), tm=1024, tn=1024)

    tm_e = 1024
    xn, top_idx, top_gate = _router(h, od_norm_ffn[0][None, :], _pad_lanes(od_router[0]),
                                    _pad_lanes(od_router_b[0][None, :]), tm=512)
    dest, row_tok, block_e, block_rows, n_used = _dispatch_plan(top_idx[:, :TOP_K], m, tm_e)
    y_rows = _moe_experts(block_e, n_used, block_rows, row_tok, xn,
                          od_moe_w1[0], od_moe_w3[0], od_moe_w2[0],
                          tm=tm_e, tf=256)
    out = _combine(dest, y_rows, h, top_gate, final_norm[None, :], tc=256)
    return out.reshape(bsz, s, d)
```
